```python
import math
import jax, jax.numpy as jnp
from jax import lax
import numpy as np

D_MODEL = 1024
BATCH = 2
SEQ = 8192
DEPTH = 4

ATTN_HEADS = 8
ATTN_HEAD_DIM = 64
ATTN_QK_WIDTH = 2 * ATTN_HEADS * ATTN_HEAD_DIM
ATTN_V_WIDTH = ATTN_HEADS * 2 * ATTN_HEAD_DIM
Q_BLOCK = 128
GMLP_CHUNK = 128
GMLP_GROUPS = 8
GMLP_GROUP_DIM = 128
GMLP_WIDTH = GMLP_GROUPS * GMLP_GROUP_DIM
IN_WIDTH = 2 * ATTN_QK_WIDTH + ATTN_V_WIDTH + 2 * GMLP_WIDTH + 2 * D_MODEL
N_EXPERTS = 16
EXPERT_FF = 2 * D_MODEL
EC_CAPACITY = 2
NORM_EPS = 1e-6

kernel_name = 'hybrid_diffattn_gmlp_ecmoe_encoder'


def _rmsnorm(x, g):
    x32 = x.astype(jnp.float32)
    y = x32 * lax.rsqrt(jnp.mean(x32 * x32, axis=-1, keepdims=True) + NORM_EPS)
    return (y * g.astype(jnp.float32)).astype(x.dtype)


def _layernorm(x, g, b):
    x32 = x.astype(jnp.float32)
    mu = jnp.mean(x32, axis=-1, keepdims=True)
    xc = x32 - mu
    var = jnp.mean(xc * xc, axis=-1, keepdims=True)
    y = xc * lax.rsqrt(var + NORM_EPS) * g.astype(jnp.float32) + b.astype(jnp.float32)
    return y.astype(x.dtype)


def _alibi_slopes():
    return 2.0 ** (-8.0 * jnp.arange(1, ATTN_HEADS + 1, dtype=jnp.float32) / ATTN_HEADS)


def _diff_lambda(lq1, lk1, lq2, lk2, lam_init):
    f = lambda a: a.astype(jnp.float32)
    return jnp.exp(jnp.sum(f(lq1) * f(lk1))) - jnp.exp(jnp.sum(f(lq2) * f(lk2))) + lam_init


def _diff_attention(q, k, v, lam):
    B, S = q.shape[0], q.shape[1]
    nb = S // Q_BLOCK
    qb = jnp.moveaxis(q.reshape(B, nb, Q_BLOCK, 2, ATTN_HEADS, ATTN_HEAD_DIM), 1, 0)
    kpos = jnp.arange(S)
    slopes = _alibi_slopes()
    scale = ATTN_HEAD_DIM ** -0.5

    def block(args):
        qblk, i = args
        qpos = i * Q_BLOCK + jnp.arange(Q_BLOCK)
        dist = jnp.abs(qpos[:, None] - kpos[None, :]).astype(jnp.float32)
        bias = -slopes[:, None, None] * dist
        s = jnp.einsum('bqnhd,bknhd->bnhqk', qblk, k).astype(jnp.float32) * scale + bias
        p = jax.nn.softmax(s, axis=-1)
        a = p[:, 0] - lam * p[:, 1]
        return jnp.einsum('bhqk,bkhe->bqhe', a.astype(v.dtype), v)

    out = lax.map(block, (qb, jnp.arange(nb)))
    return jnp.moveaxis(out, 0, 1).reshape(B, S, ATTN_HEADS, 2 * ATTN_HEAD_DIM)


def _spatial_gating(u, vg, ln_g, ln_b, w_s, b_s):
    u = jax.nn.gelu(u, approximate=False)
    vg = _layernorm(jax.nn.gelu(vg, approximate=False), ln_g, ln_b)
    B, S, W = vg.shape
    vc = vg.reshape(B, S // GMLP_CHUNK, GMLP_CHUNK, GMLP_GROUPS, GMLP_GROUP_DIM)
    mixed = jnp.einsum('gts,bcsgk->bctgk', w_s, vc) + jnp.swapaxes(b_s, 0, 1)[None, None, :, :, None]
    return u * mixed.reshape(B, S, W)


def _mixer(h, w_in, lq1, lk1, lq2, lk2, g_subln, ln_v_g, ln_v_b, w_s, b_s, w_a, w_b, w_o, lam_init):
    B, S, _ = h.shape
    z = h @ w_in
    cuts = np.cumsum([ATTN_QK_WIDTH, ATTN_QK_WIDTH, ATTN_V_WIDTH, GMLP_WIDTH, GMLP_WIDTH, D_MODEL]).tolist()
    q, k, v, u, vg, ga, gb = jnp.split(z, cuts, axis=-1)
    q = q.reshape(B, S, 2, ATTN_HEADS, ATTN_HEAD_DIM)
    k = k.reshape(B, S, 2, ATTN_HEADS, ATTN_HEAD_DIM)
    v = v.reshape(B, S, ATTN_HEADS, 2 * ATTN_HEAD_DIM)
    lam = _diff_lambda(lq1, lk1, lq2, lk2, lam_init)
    o = _diff_attention(q, k, v, lam)
    o = _rmsnorm(o, g_subln) * (1.0 - lam_init)
    ya = o.reshape(B, S, ATTN_V_WIDTH) @ w_a
    yb = _spatial_gating(u, vg, ln_v_g, ln_v_b, w_s, b_s) @ w_b
    merged = jax.nn.sigmoid(ga) * ya + jax.nn.sigmoid(gb) * yb
    return merged @ w_o


def _expert_choice_moe(h, w_r, b_r, w_g, w_u, w_d):
    B, S, _ = h.shape
    cap = EC_CAPACITY * S // N_EXPERTS
    logits = (h @ w_r + b_r).astype(jnp.float32)
    aff = jax.nn.softmax(logits, axis=-1)
    gate, idx = lax.top_k(jnp.swapaxes(aff, 1, 2), cap)
    bidx = jnp.arange(B)[:, None, None]
    xg = h[bidx, idx]
    hid = jax.nn.silu(jnp.einsum('becd,edf->becf', xg, w_g)) * jnp.einsum('becd,edf->becf', xg, w_u)
    y = jnp.einsum('becf,efd->becd', hid, w_d) * gate[..., None].astype(h.dtype)
    return jnp.zeros_like(h).at[bidx, idx].add(y)


def _modulate(x, g, shift, scale):
    return _rmsnorm(x, g) * (1.0 + scale[:, None, :]) + shift[:, None, :]


def setup_inputs(seed: int = 0) -> dict:
    key = jax.random.key(seed)
    ks = jax.random.split(key, 26)
    L = DEPTH

    def nrm(k, shape, s):
        return jax.random.normal(k, shape, jnp.float32) * s

    return {
        'x': nrm(ks[0], (BATCH, SEQ, D_MODEL), 1.0),
        'c': nrm(ks[1], (BATCH, D_MODEL), 1.0),
        'w_ada': nrm(ks[2], (L, D_MODEL, 6 * D_MODEL), 0.25 * D_MODEL ** -0.5),
        'b_ada': nrm(ks[3], (L, 6 * D_MODEL), 0.01),
        'g_pre_mix': 1.0 + nrm(ks[4], (L, D_MODEL), 0.05),
        'g_post_mix': 1.0 + nrm(ks[5], (L, D_MODEL), 0.05),
        'w_in': nrm(ks[6], (L, D_MODEL, IN_WIDTH), D_MODEL ** -0.5),
        'lam_q1': nrm(ks[7], (L, ATTN_HEAD_DIM), 0.1),
        'lam_k1': nrm(ks[8], (L, ATTN_HEAD_DIM), 0.1),
        'lam_q2': nrm(ks[9], (L, ATTN_HEAD_DIM), 0.1),
        'lam_k2': nrm(ks[10], (L, ATTN_HEAD_DIM), 0.1),
        'g_subln': 1.0 + nrm(ks[11], (L, 2 * ATTN_HEAD_DIM), 0.05),
        'ln_v_g': 1.0 + nrm(ks[12], (L, GMLP_WIDTH), 0.05),
        'ln_v_b': nrm(ks[13], (L, GMLP_WIDTH), 0.01),
        'w_spatial': nrm(ks[14], (L, GMLP_GROUPS, GMLP_CHUNK, GMLP_CHUNK), GMLP_CHUNK ** -0.5),
        'b_spatial': 1.0 + nrm(ks[15], (L, GMLP_GROUPS, GMLP_CHUNK), 0.01),
        'w_branch_a': nrm(ks[16], (L, ATTN_V_WIDTH, D_MODEL), ATTN_V_WIDTH ** -0.5),
        'w_branch_b': nrm(ks[17], (L, GMLP_WIDTH, D_MODEL), GMLP_WIDTH ** -0.5),
        'w_out': nrm(ks[18], (L, D_MODEL, D_MODEL), D_MODEL ** -0.5),
        'g_pre_ffn': 1.0 + nrm(ks[19], (L, D_MODEL), 0.05),
        'g_post_ffn': 1.0 + nrm(ks[20], (L, D_MODEL), 0.05),
        'w_router': nrm(ks[21], (L, D_MODEL, N_EXPERTS), D_MODEL ** -0.5),
        'b_router': nrm(ks[22], (L, N_EXPERTS), 0.01),
        'w_gate_e': nrm(ks[23], (L, N_EXPERTS, D_MODEL, EXPERT_FF), D_MODEL ** -0.5),
        'w_up_e': nrm(ks[24], (L, N_EXPERTS, D_MODEL, EXPERT_FF), D_MODEL ** -0.5),
        'w_down_e': nrm(ks[25], (L, N_EXPERTS, EXPERT_FF, D_MODEL), EXPERT_FF ** -0.5),
    }


def reference(x, c, w_ada, b_ada, g_pre_mix, g_post_mix, w_in, lam_q1, lam_k1, lam_q2, lam_k2,
              g_subln, ln_v_g, ln_v_b, w_spatial, b_spatial, w_branch_a, w_branch_b, w_out,
              g_pre_ffn, g_post_ffn, w_router, b_router, w_gate_e, w_up_e, w_down_e):
    c_act = jax.nn.silu(c)
    for l in range(DEPTH):
        lam_init = 0.8 - 0.6 * math.exp(-0.3 * l)
        mod = c_act @ w_ada[l] + b_ada[l]
        sh1, sc1, gt1, sh2, sc2, gt2 = jnp.split(mod, 6, axis=-1)
        h = _modulate(x, g_pre_mix[l], sh1, sc1)
        y = _mixer(h, w_in[l], lam_q1[l], lam_k1[l], lam_q2[l], lam_k2[l], g_subln[l],
                   ln_v_g[l], ln_v_b[l], w_spatial[l], b_spatial[l],
                   w_branch_a[l], w_branch_b[l], w_out[l], lam_init)
        x = x + gt1[:, None, :] * _rmsnorm(y, g_post_mix[l])
        h = _modulate(x, g_pre_ffn[l], sh2, sc2)
        y = _expert_choice_moe(h, w_router[l], b_router[l], w_gate_e[l], w_up_e[l], w_down_e[l])
        x = x + gt2[:, None, :] * _rmsnorm(y, g_post_ffn[l])
    return x
```

```python
import functools
import math

import jax
import jax.numpy as jnp
from jax import lax
from jax.experimental import pallas as pl
from jax.experimental.pallas import tpu as pltpu

F32 = jnp.float32
BF16 = jnp.bfloat16
I32 = jnp.int32

NORM_EPS = 1e-6
ATTN_HEADS = 8
ATTN_HEAD_DIM = 64
HEAD_WIDTH = 2 * ATTN_HEAD_DIM
GMLP_CHUNK = 128
GMLP_GROUPS = 8
EC_CAPACITY = 2
LANES = 128
BF16_SUBLANES = 16
VMEM_LIMIT = 56 * 1024 * 1024


def _cparams(sem):
    return pltpu.CompilerParams(dimension_semantics=sem, vmem_limit_bytes=VMEM_LIMIT)


def _rms(x, g):
    return x * lax.rsqrt(jnp.mean(x * x, axis=-1, keepdims=True) + NORM_EPS) * g


def _gelu(x):
    return 0.5 * x * (1.0 + lax.erf(x * (2.0 ** -0.5)))


def _mod_kernel(c_ref, w_ref, b_ref, o_ref):
    c = c_ref[...]
    ca = (c * jax.nn.sigmoid(c)).astype(BF16)
    o_ref[0] = jnp.dot(ca, w_ref[0].astype(BF16), preferred_element_type=F32) + b_ref[0]


def _modulation(c, w_ada, b_ada):
    L, D, W = w_ada.shape
    B = c.shape[0]
    return pl.pallas_call(
        _mod_kernel,
        grid=(L, W // D),
        in_specs=[
            pl.BlockSpec((B, D), lambda l, j: (0, 0)),
            pl.BlockSpec((1, D, D), lambda l, j: (l, 0, j)),
            pl.BlockSpec((1, 1, D), lambda l, j: (l, 0, j)),
        ],
        out_specs=pl.BlockSpec((1, B, D), lambda l, j: (l, 0, j)),
        out_shape=jax.ShapeDtypeStruct((L, B, W), F32),
        compiler_params=_cparams(("arbitrary", "arbitrary")),
        name="modulation",
    )(c, w_ada, b_ada.reshape(L, 1, W))


def _inproj_kernel(x_ref, sh_ref, sc_ref, g_ref, w_ref, z_ref, h_scr):
    @pl.when(pl.program_id(1) == 0)
    def _():
        h = _rms(x_ref[...], g_ref[...]) * (1.0 + sc_ref[0]) + sh_ref[0]
        h_scr[...] = h.astype(BF16)

    z_ref[...] = jnp.dot(h_scr[...], w_ref[...], preferred_element_type=F32).astype(BF16)


def _inproj(x2, mod3, g, w, S):
    T, D = x2.shape
    N = w.shape[1]
    tm = min(1024, S)
    tn = 1024
    per = S // tm
    return pl.pallas_call(
        _inproj_kernel,
        grid=(T // tm, N // tn),
        in_specs=[
            pl.BlockSpec((tm, D), lambda i, j: (i, 0)),
            pl.BlockSpec((1, 1, D), lambda i, j: (i // per, 0, 0)),
            pl.BlockSpec((1, 1, D), lambda i, j: (i // per, 0, 1)),
            pl.BlockSpec((1, D), lambda i, j: (0, 0)),
            pl.BlockSpec((D, tn), lambda i, j: (0, j)),
        ],
        out_specs=pl.BlockSpec((tm, tn), lambda i, j: (i, j)),
        out_shape=jax.ShapeDtypeStruct((T, N), BF16),
        scratch_shapes=[pltpu.VMEM((tm, D), BF16)],
        compiler_params=_cparams(("arbitrary", "arbitrary")),
        name="inproj",
    )(x2, mod3, mod3, g, w)


def _attn_kernel(slope_ref, q_ref, k_ref, v_ref, lq1_ref, lk1_ref, lq2_ref, lk2_ref,
                 gsub_ref, o_ref, *, tq, tk, nk, lam_init):
    h = pl.program_id(1)
    i = pl.program_id(2)
    slope = slope_ref[h]
    q = q_ref[0]
    lane = lax.broadcasted_iota(I32, (tq, HEAD_WIDTH), 1)
    zero = jnp.zeros_like(q)
    qq = jnp.concatenate(
        [jnp.where(lane < ATTN_HEAD_DIM, q, zero), jnp.where(lane >= ATTN_HEAD_DIM, q, zero)], axis=0)
    qpos = (i * tq + lax.broadcasted_iota(I32, (tq, 1), 0)).astype(F32)

    def body(j, carry):
        m, l, acc = carry
        start = pl.multiple_of(j * tk, tk)
        kk = k_ref[0, pl.ds(start, tk), :]
        vv = v_ref[0, pl.ds(start, tk), :]
        s = lax.dot_general(qq, kk, (((1,), (1,)), ((), ())), preferred_element_type=F32)
        kpos = (j * tk + lax.broadcasted_iota(I32, (1, tk), 1)).astype(F32)
        bias = -slope * jnp.abs(qpos - kpos)
        s = (s.reshape(2, tq, tk) + bias[None]).reshape(2 * tq, tk)
        m_new = jnp.maximum(m, jnp.max(s, axis=1, keepdims=True))
        p = jnp.exp(s - m_new)
        alpha = jnp.exp(m - m_new)
        l = alpha * l + jnp.sum(p, axis=1, keepdims=True)
        acc = alpha * acc + jnp.dot(p.astype(BF16), vv, preferred_element_type=F32)
        return m_new, l, acc

    m0 = jnp.full((2 * tq, 1), -jnp.inf, F32)
    l0 = jnp.zeros((2 * tq, 1), F32)
    a0 = jnp.zeros((2 * tq, HEAD_WIDTH), F32)
    _, l, acc = lax.fori_loop(0, nk, body, (m0, l0, a0))
    o = acc / l
    lam = (jnp.exp(jnp.sum(lq1_ref[...] * lk1_ref[...], axis=1, keepdims=True))
           - jnp.exp(jnp.sum(lq2_ref[...] * lk2_ref[...], axis=1, keepdims=True)) + lam_init)
    od = o[:tq] - lam * o[tq:]
    o_ref[0] = (_rms(od, gsub_ref[...]) * (1.0 - lam_init)).astype(BF16)


def _attention(z3, slopes, lq1, lk1, lq2, lk2, gsub, lam_init):
    B, S, _ = z3.shape
    H = ATTN_HEADS
    tq = min(512, S)
    tk = min(512, S)
    kern = functools.partial(_attn_kernel, tq=tq, tk=tk, nk=S // tk, lam_init=lam_init)
    vec = lambda n: pl.BlockSpec((1, n), lambda b, h, i: (0, 0))
    return pl.pallas_call(
        kern,
        grid=(B, H, S // tq),
        in_specs=[
            pl.BlockSpec(memory_space=pltpu.SMEM),
            pl.BlockSpec((1, tq, HEAD_WIDTH), lambda b, h, i: (b, i, h)),
            pl.BlockSpec((1, S, HEAD_WIDTH), lambda b, h, i: (b, 0, H + h)),
            pl.BlockSpec((1, S, HEAD_WIDTH), lambda b, h, i: (b, 0, 2 * H + h)),
            vec(ATTN_HEAD_DIM), vec(ATTN_HEAD_DIM), vec(ATTN_HEAD_DIM), vec(ATTN_HEAD_DIM),
            vec(HEAD_WIDTH),
        ],
        out_specs=pl.BlockSpec((1, tq, HEAD_WIDTH), lambda b, h, i: (b, i, h)),
        out_shape=jax.ShapeDtypeStruct((B, S, H * HEAD_WIDTH), BF16),
        compiler_params=_cparams(("arbitrary", "arbitrary", "arbitrary")),
        name="diff_attention",
    )(slopes, z3, z3, z3, lq1, lk1, lq2, lk2, gsub)


def _post_kernel(o_ref, u_ref, vg_ref, ga_ref, gb_ref, x_ref, gt1_ref, sh2_ref, sc2_ref,
                 lng_ref, lnb_ref, ws_ref, bs_ref, wa_ref, wb_ref, wo_ref, gpost_ref,
                 gpre_ref, wr_ref, br_ref,
                 xo_ref, h2_ref, aff_ref, mixed_scr, *, tm):
    ya = jnp.dot(o_ref[...], wa_ref[...], preferred_element_type=F32)

    u = _gelu(u_ref[...].astype(F32))
    vg = _gelu(vg_ref[...].astype(F32))
    mu = jnp.mean(vg, axis=-1, keepdims=True)
    vc = vg - mu
    var = jnp.mean(vc * vc, axis=-1, keepdims=True)
    vgn = (vc * lax.rsqrt(var + NORM_EPS) * lng_ref[...] + lnb_ref[...]).astype(BF16)
    for c in range(tm // GMLP_CHUNK):
        rows = slice(c * GMLP_CHUNK, (c + 1) * GMLP_CHUNK)
        for g in range(GMLP_GROUPS):
            cols = slice(g * LANES, (g + 1) * LANES)
            blk = jnp.dot(ws_ref[g], vgn[rows, cols], preferred_element_type=F32)
            mixed_scr[rows, cols] = blk + bs_ref[:, g:g + 1]
    yb = jnp.dot((u * mixed_scr[...]).astype(BF16), wb_ref[...], preferred_element_type=F32)

    merged = (jax.nn.sigmoid(ga_ref[...].astype(F32)) * ya
              + jax.nn.sigmoid(gb_ref[...].astype(F32)) * yb)
    y = jnp.dot(merged.astype(BF16), wo_ref[...], preferred_element_type=F32)
    x = x_ref[...] + gt1_ref[0] * _rms(y, gpost_ref[...])
    xo_ref[...] = x

    h2 = _rms(x, gpre_ref[...]) * (1.0 + sc2_ref[0]) + sh2_ref[0]
    h2_ref[...] = h2
    logits = lax.dot_general(wr_ref[...], h2.astype(BF16), (((1,), (1,)), ((), ())),
                             preferred_element_type=F32) + br_ref[...]
    e = jnp.exp(logits - jnp.max(logits, axis=0, keepdims=True))
    aff_ref[0] = e / jnp.sum(e, axis=0, keepdims=True)


def _post(o2, z, x2, mod3, lng, lnb, ws, bsT, wa, wb, wo, gpost, gpre, wrT, br, S):
    T, D = x2.shape
    E = wrT.shape[0]
    tm = min(512, S)
    per = S // tm
    W = D
    kern = functools.partial(_post_kernel, tm=tm)
    zcol = lambda k: pl.BlockSpec((tm, W), lambda i: (i, k))
    modc = lambda k: pl.BlockSpec((1, 1, D), lambda i: (i // per, 0, k))
    full = lambda a: pl.BlockSpec(a.shape, lambda i: (0,) * a.ndim)
    return pl.pallas_call(
        kern,
        grid=(T // tm,),
        in_specs=[
            pl.BlockSpec((tm, W), lambda i: (i, 0)),
            zcol(3), zcol(4), zcol(5), zcol(6),
            pl.BlockSpec((tm, D), lambda i: (i, 0)),
            modc(2), modc(3), modc(4),
            full(lng), full(lnb), full(ws), full(bsT), full(wa), full(wb), full(wo),
            full(gpost), full(gpre), full(wrT), full(br),
        ],
        out_specs=[
            pl.BlockSpec((tm, D), lambda i: (i, 0)),
            pl.BlockSpec((tm, D), lambda i: (i, 0)),
            pl.BlockSpec((1, E, tm), lambda i: (i // per, 0, i % per)),
        ],
        out_shape=[
            jax.ShapeDtypeStruct((T, D), F32),
            jax.ShapeDtypeStruct((T, D), F32),
            jax.ShapeDtypeStruct((T // S, E, S), F32),
        ],
        scratch_shapes=[pltpu.VMEM((tm, W), F32)],
        compiler_params=_cparams(("arbitrary",)),
        name="mixer_tail",
    )(o2, z, z, z, z, x2, mod3, mod3, mod3, lng, lnb, ws, bsT, wa, wb, wo, gpost, gpre, wrT, br)


def _lane_cumsum(x_ref, out_ref, n_chunks):
    r = lax.broadcasted_iota(I32, (LANES, LANES), 0)
    c = lax.broadcasted_iota(I32, (LANES, LANES), 1)
    tri = jnp.where(r <= c, 1.0, 0.0).astype(BF16)
    carry = jnp.zeros((1, 1), F32)
    for j in range(n_chunks):
        sl = slice(j * LANES, (j + 1) * LANES)
        cs = jnp.dot(x_ref[:, sl].astype(BF16), tri, preferred_element_type=F32) + carry
        out_ref[:, sl] = cs
        carry = cs[:, LANES - 1:LANES]


def _route_kernel(aff_ref, pos_ref, sg_ref, tok_ref, flag_scr, cs_scr, *, S, C):
    aff = aff_ref[0]
    bits = pltpu.bitcast(aff, I32)

    def step(it, thr):
        cand = thr | lax.shift_left(jnp.int32(1), 30 - it)
        cnt = jnp.sum((bits >= cand).astype(I32), axis=1, keepdims=True)
        return jnp.where(cnt >= C, cand, thr)

    thr = lax.fori_loop(0, 31, step, jnp.zeros((1, 1), I32))
    gt = bits > thr
    tie = bits == thr
    need = (C - jnp.sum(gt.astype(I32), axis=1, keepdims=True)).astype(F32)
    n_chunks = S // LANES
    flag_scr[...] = tie.astype(F32)
    _lane_cumsum(flag_scr, cs_scr, n_chunks)
    sel = gt | (tie & (cs_scr[...] <= need))
    flag_scr[...] = sel.astype(F32)
    _lane_cumsum(flag_scr, cs_scr, n_chunks)
    self32 = flag_scr[...]
    pos_ref[0] = (cs_scr[...] - self32).astype(I32)
    sg_ref[0] = jnp.where(sel, aff, 0.0)

    for sc in range(C // LANES):
        slot = (sc * LANES + lax.broadcasted_iota(I32, (LANES, LANES), 0)).astype(F32)

        def count(tc, acc):
            start = pl.multiple_of(tc * LANES, LANES)
            row = cs_scr[:, pl.ds(start, LANES)]
            return acc + jnp.where(row <= slot, 1.0, 0.0)

        acc = lax.fori_loop(0, n_chunks, count, jnp.zeros((LANES, LANES), F32))
        tok_ref[0, sc * LANES:(sc + 1) * LANES, :] = jnp.sum(acc, axis=1, keepdims=True).astype(I32)


def _route(aff, C):
    B, E, S = aff.shape
    R = B * E
    kern = functools.partial(_route_kernel, S=S, C=C)
    row = pl.BlockSpec((1, 1, S), lambda r: (r, 0, 0))
    pos, sg, tok = pl.pallas_call(
        kern,
        grid=(R,),
        in_specs=[row],
        out_specs=[row, row, pl.BlockSpec((1, C, 1), lambda r: (r, 0, 0))],
        out_shape=[
            jax.ShapeDtypeStruct((R, 1, S), I32),
            jax.ShapeDtypeStruct((R, 1, S), F32),
            jax.ShapeDtypeStruct((R, C, 1), I32),
        ],
        scratch_shapes=[pltpu.VMEM((1, S), F32), pltpu.VMEM((1, S), F32)],
        compiler_params=_cparams(("arbitrary",)),
        name="route",
    )(aff.reshape(R, 1, S))
    return pos.reshape(B, E, S), sg.reshape(B, E, S), tok.reshape(R, 1, C)


def _expert_kernel(tok_ref, h2_hbm, wg_ref, wu_ref, wd_ref, y_ref, xbuf, xg_scr, acc_scr, sem,
                   *, S, C, E, nf):
    r = pl.program_id(0)
    f = pl.program_id(1)
    base = (r // E) * S

    @pl.when(f == 0)
    def _():
        def issue(s, carry):
            t = tok_ref[0, 0, s]
            pltpu.make_async_copy(h2_hbm.at[pl.ds(base + t, 1), :], xbuf.at[pl.ds(s, 1), :], sem).start()
            return carry

        lax.fori_loop(0, C, issue, 0)
        pltpu.make_async_copy(h2_hbm.at[pl.ds(0, C), :], xbuf, sem).wait()
        xg_scr[...] = xbuf[...].astype(BF16)
        acc_scr[...] = jnp.zeros_like(acc_scr)

    xg = xg_scr[...]
    g = jnp.dot(xg, wg_ref[0, 0].astype(BF16), preferred_element_type=F32)
    u = jnp.dot(xg, wu_ref[0, 0].astype(BF16), preferred_element_type=F32)
    hid = (g * jax.nn.sigmoid(g) * u).astype(BF16)
    acc_scr[...] += jnp.dot(hid, wd_ref[0, 0].astype(BF16), preferred_element_type=F32)

    @pl.when(f == nf - 1)
    def _():
        y_ref[0] = acc_scr[...].astype(BF16)


def _experts(tok, h2, w_g, w_u, w_d, layer, B, S, C):
    _, E, D, F = w_g.shape
    tf = 512
    nf = F // tf
    R = B * E
    kern = functools.partial(_expert_kernel, S=S, C=C, E=E, nf=nf)
    return pl.pallas_call(
        kern,
        grid=(R, nf),
        in_specs=[
            pl.BlockSpec((1, 1, C), lambda r, f: (r, 0, 0), memory_space=pltpu.SMEM),
            pl.BlockSpec(memory_space=pl.ANY),
            pl.BlockSpec((1, 1, D, tf), lambda r, f: (layer, r % E, 0, f)),
            pl.BlockSpec((1, 1, D, tf), lambda r, f: (layer, r % E, 0, f)),
            pl.BlockSpec((1, 1, tf, D), lambda r, f: (layer, r % E, f, 0)),
        ],
        out_specs=pl.BlockSpec((1, C, D), lambda r, f: (r, 0, 0)),
        out_shape=jax.ShapeDtypeStruct((R, C, D), BF16),
        scratch_shapes=[
            pltpu.VMEM((C, D), F32),
            pltpu.VMEM((C, D), BF16),
            pltpu.VMEM((C, D), F32),
            pltpu.SemaphoreType.DMA(()),
        ],
        compiler_params=_cparams(("arbitrary", "arbitrary")),
        name="experts",
    )(tok, h2, w_g, w_u, w_d)


def _combine_kernel(starts_ref, y_hbm, pos_ref, sg_ref, x_ref, gt2_ref, gpost_ref, xo_ref,
                    win, sem, *, E, C, tt, W, nt):
    b = pl.program_id(0)
    i = pl.program_id(1)

    def copy(e):
        s0 = starts_ref[(b * E + e) * nt + i]
        s0a = jnp.minimum((s0 // BF16_SUBLANES) * BF16_SUBLANES, C - W)
        s0a = pl.multiple_of(s0a, BF16_SUBLANES)
        return s0a, pltpu.make_async_copy(y_hbm.at[b * E + e, pl.ds(s0a, W), :], win.at[e], sem.at[e])

    for e in range(E):
        copy(e)[1].start()

    acc = jnp.zeros(xo_ref.shape, F32)
    for e in range(E):
        s0a, cp = copy(e)
        cp.wait()
        slot = s0a + lax.broadcasted_iota(I32, (W, tt), 0)
        wt = jnp.where(pos_ref[0, e:e + 1, :] == slot, sg_ref[0, e:e + 1, :], 0.0).astype(BF16)
        acc = acc + lax.dot_general(wt, win[e], (((0,), (0,)), ((), ())), preferred_element_type=F32)

    xo_ref[...] = x_ref[...] + gt2_ref[0] * _rms(acc, gpost_ref[...])


def _combine_tile(C):
    return min(256, C // 2)


def _combine(starts, y, pos, sg, x2, mod3, gpost, S, C):
    T, D = x2.shape
    B, E, _ = pos.shape
    tt = _combine_tile(C)
    W = tt + BF16_SUBLANES
    nt = S // tt
    kern = functools.partial(_combine_kernel, E=E, C=C, tt=tt, W=W, nt=nt)
    return pl.pallas_call(
        kern,
        grid_spec=pltpu.PrefetchScalarGridSpec(
            num_scalar_prefetch=1,
            grid=(B, nt),
            in_specs=[
                pl.BlockSpec(memory_space=pl.ANY),
                pl.BlockSpec((1, E, tt), lambda b, i, st: (b, 0, i)),
                pl.BlockSpec((1, E, tt), lambda b, i, st: (b, 0, i)),
                pl.BlockSpec((tt, D), lambda b, i, st: (b * nt + i, 0)),
                pl.BlockSpec((1, 1, D), lambda b, i, st: (b, 0, 5)),
                pl.BlockSpec((1, D), lambda b, i, st: (0, 0)),
            ],
            out_specs=pl.BlockSpec((tt, D), lambda b, i, st: (b * nt + i, 0)),
            scratch_shapes=[
                pltpu.VMEM((E, W, D), BF16),
                pltpu.SemaphoreType.DMA((E,)),
            ],
        ),
        out_shape=jax.ShapeDtypeStruct((T, D), F32),
        compiler_params=_cparams(("arbitrary", "arbitrary")),
        name="combine",
    )(starts, y, pos, sg, x2, mod3, gpost)


def _permute_qk(w):
    D = w.shape[0]
    return w.reshape(D, 2, ATTN_HEADS, ATTN_HEAD_DIM).transpose(0, 2, 1, 3).reshape(D, -1)


def kernel(x, c, w_ada, b_ada, g_pre_mix, g_post_mix, w_in, lam_q1, lam_k1, lam_q2, lam_k2, g_subln, ln_v_g, ln_v_b, w_spatial, b_spatial, w_branch_a, w_branch_b, w_out, g_pre_ffn, g_post_ffn, w_router, b_router, w_gate_e, w_up_e, w_down_e):
    B, S, D = x.shape
    L = w_ada.shape[0]
    E = w_router.shape[-1]
    C = EC_CAPACITY * S // E
    T = B * S
    qk = 2 * ATTN_HEADS * ATTN_HEAD_DIM

    mod = _modulation(c, w_ada, b_ada)
    slopes = 2.0 ** (-8.0 * jnp.arange(1, ATTN_HEADS + 1, dtype=F32) / ATTN_HEADS)
    x2 = x.reshape(T, D)
    row = lambda v: v.reshape(1, -1)

    for l in range(L):
        lam_init = 0.8 - 0.6 * math.exp(-0.3 * l)
        mod3 = mod[l].reshape(B, 1, 6 * D)
        w = w_in[l]
        w_l = jnp.concatenate(
            [_permute_qk(w[:, :qk]) * (ATTN_HEAD_DIM ** -0.5), _permute_qk(w[:, qk:2 * qk]), w[:, 2 * qk:]],
            axis=1).astype(BF16)

        z = _inproj(x2, mod3, row(g_pre_mix[l]), w_l, S)
        o = _attention(z.reshape(B, S, -1), slopes, row(lam_q1[l]), row(lam_k1[l]), row(lam_q2[l]),
                       row(lam_k2[l]), row(g_subln[l]), lam_init)
        x2, h2, aff = _post(
            o.reshape(T, -1), z, x2, mod3, row(ln_v_g[l]), row(ln_v_b[l]),
            w_spatial[l].astype(BF16), b_spatial[l].T, w_branch_a[l].astype(BF16),
            w_branch_b[l].astype(BF16), w_out[l].astype(BF16), row(g_post_mix[l]),
            row(g_pre_ffn[l]), w_router[l].T.astype(BF16), b_router[l].reshape(E, 1), S)

        pos, sg, tok = _route(aff, C)
        y = _experts(tok, h2, w_gate_e, w_up_e, w_down_e, l, B, S, C)
        starts = pos[:, :, ::_combine_tile(C)].reshape(-1)
        x2 = _combine(starts, y, pos, sg, x2, mod3, row(g_post_ffn[l]), S, C)

    return x2.reshape(B, S, D)
```

```python
import functools
import math

import numpy as np
import jax
import jax.numpy as jnp
from jax import lax
from jax.experimental import pallas as pl
from jax.experimental.pallas import tpu as pltpu

F32 = jnp.float32
BF16 = jnp.bfloat16
I32 = jnp.int32

NORM_EPS = 1e-6
ATTN_HEADS = 8
ATTN_HEAD_DIM = 64
HEAD_WIDTH = 2 * ATTN_HEAD_DIM
GMLP_CHUNK = 128
GMLP_GROUPS = 8
EC_CAPACITY = 2
LANES = 128
BF16_SUBLANES = 16
VMEM_LIMIT = 56 * 1024 * 1024
LOG2E = 1.4426950408889634
POS_SPLIT = 64


def _bf16_split3(x):
    out = []
    r = np.float64(x)
    for _ in range(3):
        c = np.float64(np.asarray(r, np.float32).astype(BF16).astype(np.float32))
        out.append(float(c))
        r = r - c
    return out


LOG2E_BF16_PARTS = _bf16_split3(LOG2E)


def _cparams(sem):
    return pltpu.CompilerParams(dimension_semantics=sem, vmem_limit_bytes=VMEM_LIMIT)


def _rms(x, g):
    return x * lax.rsqrt(jnp.mean(x * x, axis=-1, keepdims=True) + NORM_EPS) * g


def _gelu(x):
    return 0.5 * x * (1.0 + lax.erf(x * (2.0 ** -0.5)))


def _mod_kernel(c_ref, w_ref, b_ref, o_ref):
    c = c_ref[...]
    ca = (c * jax.nn.sigmoid(c)).astype(BF16)
    o_ref[0] = jnp.dot(ca, w_ref[0].astype(BF16), preferred_element_type=F32) + b_ref[0]


def _modulation(c, w_ada, b_ada):
    L, D, W = w_ada.shape
    B = c.shape[0]
    return pl.pallas_call(
        _mod_kernel,
        grid=(L, W // D),
        in_specs=[
            pl.BlockSpec((B, D), lambda l, j: (0, 0)),
            pl.BlockSpec((1, D, D), lambda l, j: (l, 0, j)),
            pl.BlockSpec((1, 1, D), lambda l, j: (l, 0, j)),
        ],
        out_specs=pl.BlockSpec((1, B, D), lambda l, j: (l, 0, j)),
        out_shape=jax.ShapeDtypeStruct((L, B, W), F32),
        compiler_params=_cparams(("arbitrary", "arbitrary")),
        name="modulation",
    )(c, w_ada, b_ada.reshape(L, 1, W))


def _inproj_kernel(x_ref, sh_ref, sc_ref, g_ref, w_ref, z_ref, h_scr):
    @pl.when(pl.program_id(1) == 0)
    def _():
        h = _rms(x_ref[...], g_ref[...]) * (1.0 + sc_ref[0]) + sh_ref[0]
        h_scr[...] = h.astype(BF16)

    z_ref[...] = jnp.dot(h_scr[...], w_ref[...], preferred_element_type=F32).astype(BF16)


def _inproj(x2, mod3, g, w, S):
    T, D = x2.shape
    N = w.shape[1]
    tm = min(1024, S)
    tn = 1024
    per = S // tm
    return pl.pallas_call(
        _inproj_kernel,
        grid=(T // tm, N // tn),
        in_specs=[
            pl.BlockSpec((tm, D), lambda i, j: (i, 0)),
            pl.BlockSpec((1, 1, D), lambda i, j: (i // per, 0, 0)),
            pl.BlockSpec((1, 1, D), lambda i, j: (i // per, 0, 1)),
            pl.BlockSpec((1, D), lambda i, j: (0, 0)),
            pl.BlockSpec((D, tn), lambda i, j: (0, j)),
        ],
        out_specs=pl.BlockSpec((tm, tn), lambda i, j: (i, j)),
        out_shape=jax.ShapeDtypeStruct((T, N), BF16),
        scratch_shapes=[pltpu.VMEM((tm, D), BF16)],
        compiler_params=_cparams(("arbitrary", "arbitrary")),
        name="inproj",
    )(x2, mod3, mod3, g, w)


def _lane_tile(x, n):
    return jnp.concatenate([x] * n, axis=1)


def _bias_lanes(lane, base, v_hi, v_lo):
    return jnp.where((lane >= base) & (lane < base + 3), v_hi,
                     jnp.where((lane >= base + 3) & (lane < base + 6), v_lo, 0.0))


def _const_lanes(lane, base, sign):
    c = jnp.zeros(lane.shape, F32)
    for t, part in enumerate(LOG2E_BF16_PARTS):
        c = jnp.where((lane == base + t) | (lane == base + 3 + t), sign * part, c)
    return c


def _attn_kernel(slope_ref, q_ref, k_ref, v_ref, lq1_ref, lk1_ref, lq2_ref, lk2_ref, gsub_ref, o_ref,
                 ka_scr, kb_scr, va_scr, qaug_scr, m_scr, acc_scr, s_scr, p_scr, al_scr,
                 *, tq, tk, S, rb, lam_init):
    h = pl.program_id(1)
    i = pl.program_id(2)
    slope = slope_ref[h]
    slope2 = slope * LOG2E
    nk = S // tk
    D = ATTN_HEAD_DIM
    dn = (((1,), (1,)), ((), ()))
    kx = (ka_scr, kb_scr)

    @pl.when(i == 0)
    def _prep():
        pc = min(512, S)
        for c in range(S // pc):
            rows = slice(c * pc, (c + 1) * pc)
            lane = lax.broadcasted_iota(I32, (pc, LANES), 1)
            pos = c * pc + lax.broadcasted_iota(I32, (pc, LANES), 0)
            hi = ((pos // POS_SPLIT) * POS_SPLIT).astype(F32) * slope
            lo = (pos % POS_SPLIT).astype(F32) * slope
            kk = k_ref[0, rows, :].astype(F32)
            ka_scr[rows, :] = jnp.where(lane < D, kk, _bias_lanes(lane, D, hi, lo)).astype(BF16)
            kb_scr[rows, :] = jnp.where(lane >= D, kk, _bias_lanes(lane, 0, hi, lo)).astype(BF16)
            va_scr[rows, :HEAD_WIDTH] = v_ref[0, rows, :]
            va_scr[rows, HEAD_WIDTH:] = jnp.where(lane == 0, 1.0, 0.0).astype(BF16)

    q = q_ref[0].astype(F32)
    lane = lax.broadcasted_iota(I32, (tq, LANES), 1)
    qposi = i * tq + lax.broadcasted_iota(I32, (tq, LANES), 0)
    a_row = slope2 * qposi.astype(F32)

    for v, sign in enumerate((1.0, -1.0, 0.0)):
        qaug_scr[v, 0] = jnp.where(lane < D, q, _const_lanes(lane, D, sign)).astype(BF16)
        qaug_scr[v, 1] = jnp.where(lane >= D, q, _const_lanes(lane, 0, sign)).astype(BF16)
    for mp in range(2):
        m_scr[mp][...] = jnp.full(m_scr[mp].shape, -jnp.inf, F32)
        acc_scr[mp][...] = jnp.zeros(acc_scr[mp].shape, F32)

    jd = (i * tq) // tk

    def chunk_of(t):
        c = jd + t
        return jnp.where(c >= nk, c - nk, c)

    def qk(t, par, variant):
        ks = pl.ds(pl.multiple_of(chunk_of(t) * tk, tk), tk)
        for mp in range(2):
            s_scr[par][mp][...] = lax.dot_general(qaug_scr[variant, mp], kx[mp][ks, :], dn,
                                                  preferred_element_type=F32)

    def softmax(t, par, overlapping):
        c = chunk_of(t)
        osign = jnp.where(c > jd, 1.0, -1.0).astype(F32)
        for mp in range(2):
            for r in range(0, tq, rb):
                s = s_scr[par][mp][r:r + rb, :]
                if overlapping:
                    kpos = (c * tk + lax.broadcasted_iota(I32, (1, tk), 1)).astype(F32)
                    qpos = (i * tq + r + lax.broadcasted_iota(I32, (rb, 1), 0)).astype(F32)
                    s = s - slope2 * jnp.abs(qpos - kpos)
                    off = jnp.zeros((rb, LANES), F32)
                else:
                    off = osign * a_row[r:r + rb, :]
                mx = jnp.broadcast_to(jnp.max(s, axis=1, keepdims=True), (rb, LANES))
                m_old = m_scr[mp][r:r + rb, :]
                m_new = jnp.maximum(m_old, mx + off)
                p = jnp.exp2(s - _lane_tile(m_new - off, tk // LANES))
                p_scr[par][mp][r:r + rb, :] = p.astype(BF16)
                al_scr[par][mp][r:r + rb, :] = jnp.exp2(m_old - m_new)
                m_scr[mp][r:r + rb, :] = m_new

    def pv(t, par):
        ks = pl.ds(pl.multiple_of(chunk_of(t) * tk, tk), tk)
        for mp in range(2):
            upd = jnp.dot(p_scr[par][mp][...], va_scr[ks, :], preferred_element_type=F32)
            acc_scr[mp][...] = _lane_tile(al_scr[par][mp][...], 2) * acc_scr[mp][...] + upd

    def variant_of(t):
        return jnp.where(chunk_of(t) > jd, 1, 0)

    qk(0, 0, 2)
    qk(1, 1, variant_of(1))
    softmax(0, 0, True)

    def tick_pair(u, carry):
        t = 2 * u + 2
        qk(t, 0, variant_of(t))
        softmax(t - 1, 1, False)
        pv(t - 2, 0)
        qk(t + 1, 1, variant_of(t + 1))
        softmax(t, 0, False)
        pv(t - 1, 1)
        return carry

    lax.fori_loop(0, (nk - 2) // 2, tick_pair, 0)

    softmax(nk - 1, 1, False)
    pv(nk - 2, 0)
    pv(nk - 1, 1)

    lam = (jnp.exp(jnp.sum(lq1_ref[...] * lk1_ref[...], axis=1, keepdims=True))
           - jnp.exp(jnp.sum(lq2_ref[...] * lk2_ref[...], axis=1, keepdims=True)) + lam_init)
    a1 = acc_scr[0][...]
    a2 = acc_scr[1][...]
    o1 = a1[:, :HEAD_WIDTH] / a1[:, HEAD_WIDTH:HEAD_WIDTH + 1]
    o2 = a2[:, :HEAD_WIDTH] / a2[:, HEAD_WIDTH:HEAD_WIDTH + 1]
    od = o1 - lam * o2
    o_ref[0] = (_rms(od, gsub_ref[...]) * (1.0 - lam_init)).astype(BF16)


def _attn_tiles(S):
    tq = min(512, S // 2)
    tk = min(1024, S // 2)
    return tq, tk, 32


def _attention(z3, slopes, lq1, lk1, lq2, lk2, gsub, lam_init):
    B, S, _ = z3.shape
    H = ATTN_HEADS
    tq, tk, rb = _attn_tiles(S)
    assert S % tk == 0 and (S // tk) % 2 == 0 and tk % tq == 0 and tq % rb == 0
    kern = functools.partial(_attn_kernel, tq=tq, tk=tk, S=S, rb=rb, lam_init=lam_init)
    vec = lambda n: pl.BlockSpec((1, n), lambda b, h, i: (0, 0))
    pair = lambda shape, dt: [pltpu.VMEM(shape, dt) for _ in range(2)]
    return pl.pallas_call(
        kern,
        grid=(B, H, S // tq),
        in_specs=[
            pl.BlockSpec(memory_space=pltpu.SMEM),
            pl.BlockSpec((1, tq, HEAD_WIDTH), lambda b, h, i: (b, i, h)),
            pl.BlockSpec((1, S, HEAD_WIDTH), lambda b, h, i: (b, 0, H + h)),
            pl.BlockSpec((1, S, HEAD_WIDTH), lambda b, h, i: (b, 0, 2 * H + h)),
            vec(ATTN_HEAD_DIM), vec(ATTN_HEAD_DIM), vec(ATTN_HEAD_DIM), vec(ATTN_HEAD_DIM),
            vec(HEAD_WIDTH),
        ],
        out_specs=pl.BlockSpec((1, tq, HEAD_WIDTH), lambda b, h, i: (b, i, h)),
        out_shape=jax.ShapeDtypeStruct((B, S, H * HEAD_WIDTH), BF16),
        scratch_shapes=[
            pltpu.VMEM((S, LANES), BF16),
            pltpu.VMEM((S, LANES), BF16),
            pltpu.VMEM((S, 2 * LANES), BF16),
            pltpu.VMEM((3, 2, tq, LANES), BF16),
            pair((tq, LANES), F32),
            pair((tq, 2 * LANES), F32),
            [pair((tq, tk), F32) for _ in range(2)],
            [pair((tq, tk), BF16) for _ in range(2)],
            [pair((tq, LANES), F32) for _ in range(2)],
        ],
        compiler_params=_cparams(("arbitrary", "arbitrary", "arbitrary")),
        name="diff_attention",
    )(slopes, z3, z3, z3, lq1, lk1, lq2, lk2, gsub)


def _post_kernel(o_ref, u_ref, vg_ref, ga_ref, gb_ref, x_ref, gt1_ref, sh2_ref, sc2_ref,
                 lng_ref, lnb_ref, ws_ref, bs_ref, wa_ref, wb_ref, wo_ref, gpost_ref,
                 gpre_ref, wr_ref, br_ref,
                 xo_ref, h2_ref, aff_ref, mixed_scr, *, tm):
    ya = jnp.dot(o_ref[...], wa_ref[...], preferred_element_type=F32)

    u = _gelu(u_ref[...].astype(F32))
    vg = _gelu(vg_ref[...].astype(F32))
    mu = jnp.mean(vg, axis=-1, keepdims=True)
    vc = vg - mu
    var = jnp.mean(vc * vc, axis=-1, keepdims=True)
    vgn = (vc * lax.rsqrt(var + NORM_EPS) * lng_ref[...] + lnb_ref[...]).astype(BF16)
    for c in range(tm // GMLP_CHUNK):
        rows = slice(c * GMLP_CHUNK, (c + 1) * GMLP_CHUNK)
        for g in range(GMLP_GROUPS):
            cols = slice(g * LANES, (g + 1) * LANES)
            blk = jnp.dot(ws_ref[g], vgn[rows, cols], preferred_element_type=F32)
            mixed_scr[rows, cols] = blk + bs_ref[:, g:g + 1]
    yb = jnp.dot((u * mixed_scr[...]).astype(BF16), wb_ref[...], preferred_element_type=F32)

    merged = (jax.nn.sigmoid(ga_ref[...].astype(F32)) * ya
              + jax.nn.sigmoid(gb_ref[...].astype(F32)) * yb)
    y = jnp.dot(merged.astype(BF16), wo_ref[...], preferred_element_type=F32)
    x = x_ref[...] + gt1_ref[0] * _rms(y, gpost_ref[...])
    xo_ref[...] = x

    h2 = _rms(x, gpre_ref[...]) * (1.0 + sc2_ref[0]) + sh2_ref[0]
    h2_ref[...] = h2
    logits = lax.dot_general(wr_ref[...], h2.astype(BF16), (((1,), (1,)), ((), ())),
                             preferred_element_type=F32) + br_ref[...]
    e = jnp.exp(logits - jnp.max(logits, axis=0, keepdims=True))
    aff_ref[0] = e / jnp.sum(e, axis=0, keepdims=True)


def _post(o2, z, x2, mod3, lng, lnb, ws, bsT, wa, wb, wo, gpost, gpre, wrT, br, S):
    T, D = x2.shape
    E = wrT.shape[0]
    tm = min(512, S)
    per = S // tm
    W = D
    kern = functools.partial(_post_kernel, tm=tm)
    zcol = lambda k: pl.BlockSpec((tm, W), lambda i: (i, k))
    modc = lambda k: pl.BlockSpec((1, 1, D), lambda i: (i // per, 0, k))
    full = lambda a: pl.BlockSpec(a.shape, lambda i: (0,) * a.ndim)
    return pl.pallas_call(
        kern,
        grid=(T // tm,),
        in_specs=[
            pl.BlockSpec((tm, W), lambda i: (i, 0)),
            zcol(3), zcol(4), zcol(5), zcol(6),
            pl.BlockSpec((tm, D), lambda i: (i, 0)),
            modc(2), modc(3), modc(4),
            full(lng), full(lnb), full(ws), full(bsT), full(wa), full(wb), full(wo),
            full(gpost), full(gpre), full(wrT), full(br),
        ],
        out_specs=[
            pl.BlockSpec((tm, D), lambda i: (i, 0)),
            pl.BlockSpec((tm, D), lambda i: (i, 0)),
            pl.BlockSpec((1, E, tm), lambda i: (i // per, 0, i % per)),
        ],
        out_shape=[
            jax.ShapeDtypeStruct((T, D), F32),
            jax.ShapeDtypeStruct((T, D), F32),
            jax.ShapeDtypeStruct((T // S, E, S), F32),
        ],
        scratch_shapes=[pltpu.VMEM((tm, W), F32)],
        compiler_params=_cparams(("arbitrary",)),
        name="mixer_tail",
    )(o2, z, z, z, z, x2, mod3, mod3, mod3, lng, lnb, ws, bsT, wa, wb, wo, gpost, gpre, wrT, br)


def _seq_cumsum(x):
    nc = x.shape[0]
    r = lax.broadcasted_iota(I32, (LANES, LANES), 0)
    c = lax.broadcasted_iota(I32, (LANES, LANES), 1)
    tri = jnp.where(r <= c, 1.0, 0.0).astype(BF16)
    within = jnp.dot(x.astype(BF16), tri, preferred_element_type=F32)
    totals = jnp.broadcast_to(within[:, LANES - 1:LANES], (nc, LANES)).astype(BF16)
    rr = lax.broadcasted_iota(I32, (nc, nc), 0)
    cc = lax.broadcasted_iota(I32, (nc, nc), 1)
    before = jnp.where(cc < rr, 1.0, 0.0).astype(BF16)
    return within + jnp.dot(before, totals, preferred_element_type=F32)


def _route_kernel(aff_ref, pos_ref, sg_ref, tok_ref, cs_scr, *, C):
    aff = aff_ref[0]
    nc = aff.shape[0]
    bits = pltpu.bitcast(aff, I32)

    def count(mask):
        return jnp.sum(jnp.sum(mask.astype(I32), axis=1, keepdims=True), axis=0, keepdims=True)

    def step(it, thr):
        cand = thr | lax.shift_left(jnp.int32(1), 30 - it)
        return jnp.where(count(bits >= cand) >= C, cand, thr)

    thr = lax.fori_loop(0, 31, step, jnp.zeros((1, 1), I32))
    gt = bits > thr
    tie = bits == thr
    need = (C - count(gt)).astype(F32)
    sel = gt | (tie & (_seq_cumsum(tie.astype(F32)) <= need))
    self32 = sel.astype(F32)
    cs = _seq_cumsum(self32)
    cs_scr[...] = cs
    pos_ref[0] = (cs - self32).astype(I32)
    sg_ref[0] = jnp.where(sel, aff, 0.0)

    for sc in range(C // LANES):
        slot = (sc * LANES + lax.broadcasted_iota(I32, (LANES, LANES), 0)).astype(F32)

        def count_chunk(tc, acc):
            return acc + jnp.where(cs_scr[pl.ds(tc, 1), :] <= slot, 1.0, 0.0)

        acc = lax.fori_loop(0, nc, count_chunk, jnp.zeros((LANES, LANES), F32))
        tok_ref[0, sc * LANES:(sc + 1) * LANES, :] = jnp.sum(acc, axis=1, keepdims=True).astype(I32)


def _route(aff, C):
    B, E, S = aff.shape
    R = B * E
    nc = S // LANES
    kern = functools.partial(_route_kernel, C=C)
    row = pl.BlockSpec((1, nc, LANES), lambda r: (r, 0, 0))
    pos, sg, tok = pl.pallas_call(
        kern,
        grid=(R,),
        in_specs=[row],
        out_specs=[row, row, pl.BlockSpec((1, C, 1), lambda r: (r, 0, 0))],
        out_shape=[
            jax.ShapeDtypeStruct((R, nc, LANES), I32),
            jax.ShapeDtypeStruct((R, nc, LANES), F32),
            jax.ShapeDtypeStruct((R, C, 1), I32),
        ],
        scratch_shapes=[pltpu.VMEM((nc, LANES), F32)],
        compiler_params=_cparams(("arbitrary",)),
        name="route",
    )(aff.reshape(R, nc, LANES))
    return pos.reshape(B, E, S), sg.reshape(B, E, S), tok.reshape(R, 1, C)


def _expert_kernel(tok_ref, tokn_ref, h2_hbm, wg_ref, wu_ref, wd_ref, y_ref, xbuf, xg_scr, acc_scr, sem,
                   *, S, C, E, R, nf):
    r = pl.program_id(0)
    f = pl.program_id(1)
    per = C // nf

    def row_copy(tok_smem, pair, s):
        t = tok_smem[0, 0, s]
        return pltpu.make_async_copy(h2_hbm.at[pl.ds((pair // E) * S + t, 1), :],
                                     xbuf.at[pl.ds(s, 1), :], sem)

    def wait_all_rows():
        pltpu.make_async_copy(h2_hbm.at[pl.ds(0, C), :], xbuf, sem).wait()

    @pl.when((r == 0) & (f == 0))
    def _():
        def issue(s, carry):
            row_copy(tok_ref, r, s).start()
            return carry

        lax.fori_loop(0, C, issue, 0)

    @pl.when(f == 0)
    def _():
        wait_all_rows()
        xg_scr[...] = xbuf[...].astype(BF16)
        acc_scr[...] = jnp.zeros_like(acc_scr)

    nxt = jnp.minimum(r + 1, R - 1)
    for s in range(per):
        row_copy(tokn_ref, nxt, f * per + s).start()

    xg = xg_scr[...]
    g = jnp.dot(xg, wg_ref[0, 0].astype(BF16), preferred_element_type=F32)
    u = jnp.dot(xg, wu_ref[0, 0].astype(BF16), preferred_element_type=F32)
    hid = (g * jax.nn.sigmoid(g) * u).astype(BF16)
    acc_scr[...] += jnp.dot(hid, wd_ref[0, 0].astype(BF16), preferred_element_type=F32)

    @pl.when(f == nf - 1)
    def _():
        y_ref[0] = acc_scr[...].astype(BF16)

    @pl.when((r == R - 1) & (f == nf - 1))
    def _():
        wait_all_rows()


def _experts(tok, h2, w_g, w_u, w_d, layer, B, S, C):
    _, E, D, F = w_g.shape
    tf = 512
    nf = F // tf
    R = B * E
    kern = functools.partial(_expert_kernel, S=S, C=C, E=E, R=R, nf=nf)
    return pl.pallas_call(
        kern,
        grid=(R, nf),
        in_specs=[
            pl.BlockSpec((1, 1, C), lambda r, f: (r, 0, 0), memory_space=pltpu.SMEM),
            pl.BlockSpec((1, 1, C), lambda r, f: (jnp.minimum(r + 1, R - 1), 0, 0),
                         memory_space=pltpu.SMEM),
            pl.BlockSpec(memory_space=pl.ANY),
            pl.BlockSpec((1, 1, D, tf), lambda r, f: (layer, r % E, 0, f)),
            pl.BlockSpec((1, 1, D, tf), lambda r, f: (layer, r % E, 0, f)),
            pl.BlockSpec((1, 1, tf, D), lambda r, f: (layer, r % E, f, 0)),
        ],
        out_specs=pl.BlockSpec((1, C, D), lambda r, f: (r, 0, 0)),
        out_shape=jax.ShapeDtypeStruct((R, C, D), BF16),
        scratch_shapes=[
            pltpu.VMEM((C, D), F32),
            pltpu.VMEM((C, D), BF16),
            pltpu.VMEM((C, D), F32),
            pltpu.SemaphoreType.DMA(()),
        ],
        compiler_params=_cparams(("arbitrary", "arbitrary")),
        name="experts",
    )(tok, tok, h2, w_g, w_u, w_d)


def _combine_kernel(starts_ref, y_hbm, pos_ref, sg_ref, x_ref, gt2_ref, gpost_ref, xo_ref,
                    win, sem, *, E, C, tt, W, nt, n_steps):
    b = pl.program_id(0)
    i = pl.program_id(1)
    step = b * nt + i

    def copy(n, e):
        bb = n // nt
        s0 = starts_ref[(bb * E + e) * nt + (n - bb * nt)]
        s0a = jnp.minimum((s0 // BF16_SUBLANES) * BF16_SUBLANES, C - W)
        s0a = pl.multiple_of(s0a, BF16_SUBLANES)
        slot = n % 2
        return s0a, pltpu.make_async_copy(y_hbm.at[bb * E + e, pl.ds(s0a, W), :], win.at[slot, e],
                                          sem.at[slot, e])

    @pl.when(step == 0)
    def _():
        for e in range(E):
            copy(step, e)[1].start()

    @pl.when(step + 1 < n_steps)
    def _():
        for e in range(E):
            copy(step + 1, e)[1].start()

    acc = jnp.zeros(xo_ref.shape, F32)
    for e in range(E):
        s0a, cp = copy(step, e)
        cp.wait()
        slot = s0a + lax.broadcasted_iota(I32, (W, tt), 0)
        wt = jnp.where(pos_ref[0, e:e + 1, :] == slot, sg_ref[0, e:e + 1, :], 0.0).astype(BF16)
        acc = acc + lax.dot_general(wt, win[step % 2, e], (((0,), (0,)), ((), ())),
                                    preferred_element_type=F32)

    xo_ref[...] = x_ref[...] + gt2_ref[0] * _rms(acc, gpost_ref[...])


def _combine_tile(C):
    return min(128, C // 2)


def _combine(starts, y, pos, sg, x2, mod3, gpost, S, C):
    T, D = x2.shape
    B, E, _ = pos.shape
    tt = _combine_tile(C)
    W = tt + BF16_SUBLANES
    nt = S // tt
    kern = functools.partial(_combine_kernel, E=E, C=C, tt=tt, W=W, nt=nt, n_steps=B * nt)
    return pl.pallas_call(
        kern,
        grid_spec=pltpu.PrefetchScalarGridSpec(
            num_scalar_prefetch=1,
            grid=(B, nt),
            in_specs=[
                pl.BlockSpec(memory_space=pl.ANY),
                pl.BlockSpec((1, E, tt), lambda b, i, st: (b, 0, i)),
                pl.BlockSpec((1, E, tt), lambda b, i, st: (b, 0, i)),
                pl.BlockSpec((tt, D), lambda b, i, st: (b * nt + i, 0)),
                pl.BlockSpec((1, 1, D), lambda b, i, st: (b, 0, 5)),
                pl.BlockSpec((1, D), lambda b, i, st: (0, 0)),
            ],
            out_specs=pl.BlockSpec((tt, D), lambda b, i, st: (b * nt + i, 0)),
            scratch_shapes=[
                pltpu.VMEM((2, E, W, D), BF16),
                pltpu.SemaphoreType.DMA((2, E)),
            ],
        ),
        out_shape=jax.ShapeDtypeStruct((T, D), F32),
        compiler_params=_cparams(("arbitrary", "arbitrary")),
        name="combine",
    )(starts, y, pos, sg, x2, mod3, gpost)


def _permute_qk(w):
    D = w.shape[0]
    return w.reshape(D, 2, ATTN_HEADS, ATTN_HEAD_DIM).transpose(0, 2, 1, 3).reshape(D, -1)


def kernel(x, c, w_ada, b_ada, g_pre_mix, g_post_mix, w_in, lam_q1, lam_k1, lam_q2, lam_k2, g_subln, ln_v_g, ln_v_b, w_spatial, b_spatial, w_branch_a, w_branch_b, w_out, g_pre_ffn, g_post_ffn, w_router, b_router, w_gate_e, w_up_e, w_down_e):
    B, S, D = x.shape
    L = w_ada.shape[0]
    E = w_router.shape[-1]
    C = EC_CAPACITY * S // E
    T = B * S
    qk = 2 * ATTN_HEADS * ATTN_HEAD_DIM

    mod = _modulation(c, w_ada, b_ada)
    slopes = 2.0 ** (-8.0 * jnp.arange(1, ATTN_HEADS + 1, dtype=F32) / ATTN_HEADS)
    x2 = x.reshape(T, D)
    row = lambda v: v.reshape(1, -1)

    for l in range(L):
        lam_init = 0.8 - 0.6 * math.exp(-0.3 * l)
        mod3 = mod[l].reshape(B, 1, 6 * D)
        w = w_in[l]
        w_l = jnp.concatenate(
            [_permute_qk(w[:, :qk]) * (LOG2E * ATTN_HEAD_DIM ** -0.5), _permute_qk(w[:, qk:2 * qk]),
             w[:, 2 * qk:]],
            axis=1).astype(BF16)

        z = _inproj(x2, mod3, row(g_pre_mix[l]), w_l, S)
        o = _attention(z.reshape(B, S, -1), slopes, row(lam_q1[l]), row(lam_k1[l]), row(lam_q2[l]),
                       row(lam_k2[l]), row(g_subln[l]), lam_init)
        x2, h2, aff = _post(
            o.reshape(T, -1), z, x2, mod3, row(ln_v_g[l]), row(ln_v_b[l]),
            w_spatial[l].astype(BF16), b_spatial[l].T, w_branch_a[l].astype(BF16),
            w_branch_b[l].astype(BF16), w_out[l].astype(BF16), row(g_post_mix[l]),
            row(g_pre_ffn[l]), w_router[l].T.astype(BF16), b_router[l].reshape(E, 1), S)

        pos, sg, tok = _route(aff, C)
        y = _experts(tok, h2, w_gate_e, w_up_e, w_down_e, l, B, S, C)
        starts = pos[:, :, ::_combine_tile(C)].reshape(-1)
        x2 = _combine(starts, y, pos, sg, x2, mod3, row(g_post_ffn[l]), S, C)

    return x2.reshape(B, S, D)
```

```python
import functools
import math

import numpy as np
import jax
import jax.numpy as jnp
from jax import lax
from jax.experimental import pallas as pl
from jax.experimental.pallas import tpu as pltpu

F32 = jnp.float32
BF16 = jnp.bfloat16
I32 = jnp.int32

NORM_EPS = 1e-6
ATTN_HEADS = 8
ATTN_HEAD_DIM = 64
HEAD_WIDTH = 2 * ATTN_HEAD_DIM
GMLP_CHUNK = 128
GMLP_GROUPS = 8
EC_CAPACITY = 2
LANES = 128
BF16_SUBLANES = 16
SUBLANE_SHIFT = 4
VMEM_LIMIT = 56 * 1024 * 1024
LOG2E = 1.4426950408889634
POS_SPLIT = 64


def _bf16_split3(x):
    out = []
    r = np.float64(x)
    for _ in range(3):
        c = np.float64(np.asarray(r, np.float32).astype(BF16).astype(np.float32))
        out.append(float(c))
        r = r - c
    return out


LOG2E_BF16_PARTS = _bf16_split3(LOG2E)


def _cparams(sem):
    return pltpu.CompilerParams(dimension_semantics=sem, vmem_limit_bytes=VMEM_LIMIT)


def _rms(x, g):
    return x * lax.rsqrt(jnp.mean(x * x, axis=-1, keepdims=True) + NORM_EPS) * g


def _gelu(x):
    return 0.5 * x * (1.0 + lax.erf(x * (2.0 ** -0.5)))


def _mod_kernel(c_ref, w_ref, b_ref, o_ref):
    c = c_ref[...]
    ca = (c * jax.nn.sigmoid(c)).astype(BF16)
    o_ref[0] = jnp.dot(ca, w_ref[0].astype(BF16), preferred_element_type=F32) + b_ref[0]


def _modulation(c, w_ada, b_ada):
    L, D, W = w_ada.shape
    B = c.shape[0]
    return pl.pallas_call(
        _mod_kernel,
        grid=(L, W // D),
        in_specs=[
            pl.BlockSpec((B, D), lambda l, j: (0, 0)),
            pl.BlockSpec((1, D, D), lambda l, j: (l, 0, j)),
            pl.BlockSpec((1, 1, D), lambda l, j: (l, 0, j)),
        ],
        out_specs=pl.BlockSpec((1, B, D), lambda l, j: (l, 0, j)),
        out_shape=jax.ShapeDtypeStruct((L, B, W), F32),
        compiler_params=_cparams(("arbitrary", "arbitrary")),
        name="modulation",
    )(c, w_ada, b_ada.reshape(L, 1, W))


def _inproj_kernel(x_ref, sh_ref, sc_ref, g_ref, w_ref, z_ref, h_scr):
    @pl.when(pl.program_id(1) == 0)
    def _():
        h = _rms(x_ref[...], g_ref[...]) * (1.0 + sc_ref[0]) + sh_ref[0]
        h_scr[...] = h.astype(BF16)

    z_ref[...] = jnp.dot(h_scr[...], w_ref[...], preferred_element_type=F32).astype(BF16)


def _inproj(x2, mod3, g, w, S):
    T, D = x2.shape
    N = w.shape[1]
    tm = min(1024, S)
    tn = 1024
    per = S // tm
    return pl.pallas_call(
        _inproj_kernel,
        grid=(T // tm, N // tn),
        in_specs=[
            pl.BlockSpec((tm, D), lambda i, j: (i, 0)),
            pl.BlockSpec((1, 1, D), lambda i, j: (i // per, 0, 0)),
            pl.BlockSpec((1, 1, D), lambda i, j: (i // per, 0, 1)),
            pl.BlockSpec((1, D), lambda i, j: (0, 0)),
            pl.BlockSpec((D, tn), lambda i, j: (0, j)),
        ],
        out_specs=pl.BlockSpec((tm, tn), lambda i, j: (i, j)),
        out_shape=jax.ShapeDtypeStruct((T, N), BF16),
        scratch_shapes=[pltpu.VMEM((tm, D), BF16)],
        compiler_params=_cparams(("arbitrary", "arbitrary")),
        name="inproj",
    )(x2, mod3, mod3, g, w)


def _lane_tile(x, n):
    return jnp.concatenate([x] * n, axis=1)


def _bias_lanes(lane, base, v_hi, v_lo):
    return jnp.where((lane >= base) & (lane < base + 3), v_hi,
                     jnp.where((lane >= base + 3) & (lane < base + 6), v_lo, 0.0))


def _const_lanes(lane, base, sign):
    c = jnp.zeros(lane.shape, F32)
    for t, part in enumerate(LOG2E_BF16_PARTS):
        c = jnp.where((lane == base + t) | (lane == base + 3 + t), sign * part, c)
    return c


def _attn_kernel(slope_ref, lam0_ref, q_ref, k_ref, v_ref, lq1_ref, lk1_ref, lq2_ref, lk2_ref, gsub_ref,
                 o_ref, ka_scr, kb_scr, va_scr, qaug_scr, m_scr, acc_scr, s_scr, p_scr, al_scr,
                 *, tq, tk, S, rb):
    h = pl.program_id(1)
    slope = slope_ref[h]
    slope2 = slope * LOG2E
    nk = S // tk
    nq = S // tq
    D = ATTN_HEAD_DIM
    dn = (((1,), (1,)), ((), ()))
    kx = (ka_scr, kb_scr)

    pc = min(512, S)
    for c in range(S // pc):
        rows = slice(c * pc, (c + 1) * pc)
        lane = lax.broadcasted_iota(I32, (pc, LANES), 1)
        pos = c * pc + lax.broadcasted_iota(I32, (pc, LANES), 0)
        hi = ((pos // POS_SPLIT) * POS_SPLIT).astype(F32) * slope
        lo = (pos % POS_SPLIT).astype(F32) * slope
        kk = k_ref[0, rows, :].astype(F32)
        ka_scr[rows, :] = jnp.where(lane < D, kk, _bias_lanes(lane, D, hi, lo)).astype(BF16)
        kb_scr[rows, :] = jnp.where(lane >= D, kk, _bias_lanes(lane, 0, hi, lo)).astype(BF16)
        va_scr[rows, :HEAD_WIDTH] = v_ref[0, rows, :]
        va_scr[rows, HEAD_WIDTH:] = jnp.where(lane == 0, 1.0, 0.0).astype(BF16)

    lam_init = lam0_ref[0]
    lam = (jnp.exp(jnp.sum(lq1_ref[...] * lk1_ref[...], axis=1, keepdims=True))
           - jnp.exp(jnp.sum(lq2_ref[...] * lk2_ref[...], axis=1, keepdims=True)) + lam_init)

    def tile_rows(qt):
        return pl.ds(pl.multiple_of(qt * tq, tq), tq)

    def first_chunk(qt):
        return lax.div(qt, tk // tq)

    def chunk_of(qt, t):
        c = first_chunk(qt) + t
        return jnp.where(c >= nk, c - nk, c)

    def variant_of(qt, t):
        return jnp.where(chunk_of(qt, t) > first_chunk(qt), 1, 0)

    def build_q(qt):
        q = q_ref[0, tile_rows(qt), :].astype(F32)
        lane = lax.broadcasted_iota(I32, (tq, LANES), 1)
        for v, sign in enumerate((1.0, -1.0, 0.0)):
            qaug_scr[v, 0] = jnp.where(lane < D, q, _const_lanes(lane, D, sign)).astype(BF16)
            qaug_scr[v, 1] = jnp.where(lane >= D, q, _const_lanes(lane, 0, sign)).astype(BF16)

    def qk(qt, t, variant):
        ks = pl.ds(pl.multiple_of(chunk_of(qt, t) * tk, tk), tk)
        for mp in range(2):
            s_scr[t % 2][mp][...] = lax.dot_general(qaug_scr[variant, mp], kx[mp][ks, :], dn,
                                                    preferred_element_type=F32)

    def softmax(qt, t):
        par = t % 2
        c = chunk_of(qt, t)
        osign = jnp.where(c > first_chunk(qt), 1.0, -1.0).astype(F32)
        row0 = qt * tq
        for mp in range(2):
            for r in range(0, tq, rb):
                s = s_scr[par][mp][r:r + rb, :]
                if t == 0:
                    kpos = (c * tk + lax.broadcasted_iota(I32, (1, tk), 1)).astype(F32)
                    qpos = (row0 + r + lax.broadcasted_iota(I32, (rb, 1), 0)).astype(F32)
                    s = s - slope2 * jnp.abs(qpos - kpos)
                    m_new = jnp.broadcast_to(jnp.max(s, axis=1, keepdims=True), (rb, LANES))
                    shift = m_new
                    al_scr[par][mp][r:r + rb, :] = jnp.zeros((rb, LANES), F32)
                else:
                    qposr = (row0 + r + lax.broadcasted_iota(I32, (rb, LANES), 0)).astype(F32)
                    off = (osign * slope2) * qposr
                    mx = jnp.broadcast_to(jnp.max(s, axis=1, keepdims=True), (rb, LANES))
                    m_old = m_scr[mp][r:r + rb, :]
                    m_new = jnp.maximum(m_old, mx + off)
                    shift = m_new - off
                    al_scr[par][mp][r:r + rb, :] = jnp.exp2(m_old - m_new)
                p = jnp.exp2(s - _lane_tile(shift, tk // LANES))
                p_scr[par][mp][r:r + rb, :] = p.astype(BF16)
                m_scr[mp][r:r + rb, :] = m_new

    def pv(qt, t):
        par = t % 2
        ks = pl.ds(pl.multiple_of(chunk_of(qt, t) * tk, tk), tk)
        for mp in range(2):
            upd = jnp.dot(p_scr[par][mp][...], va_scr[ks, :], preferred_element_type=F32)
            if t == 0:
                acc_scr[mp][...] = upd
            else:
                acc_scr[mp][...] = _lane_tile(al_scr[par][mp][...], 2) * acc_scr[mp][...] + upd

    def finalize(qt):
        a1 = acc_scr[0][...]
        a2 = acc_scr[1][...]
        o1 = a1[:, :HEAD_WIDTH] / a1[:, HEAD_WIDTH:HEAD_WIDTH + 1]
        o2 = a2[:, :HEAD_WIDTH] / a2[:, HEAD_WIDTH:HEAD_WIDTH + 1]
        od = o1 - lam * o2
        o_ref[0, tile_rows(qt), :] = (_rms(od, gsub_ref[...]) * (1.0 - lam_init)).astype(BF16)

    def tile(qt, has_prev):
        build_q(qt)
        for t in range(nk):
            qk(qt, t, 2 if t == 0 else variant_of(qt, t))
            if t >= 1:
                softmax(qt, t - 1)
            elif has_prev:
                softmax(qt - 1, nk - 1)
            if t >= 2:
                pv(qt, t - 2)
            elif has_prev:
                pv(qt - 1, nk - 2 + t)
                if t == 1:
                    finalize(qt - 1)

    tile(jnp.int32(0), False)

    def body(qt, carry):
        tile(qt, True)
        return carry

    lax.fori_loop(1, nq, body, 0)
    softmax(nq - 1, nk - 1)
    pv(nq - 1, nk - 2)
    pv(nq - 1, nk - 1)
    finalize(nq - 1)


def _attn_tiles(S):
    tk = min(1024, S // 4)
    tq = min(512, tk)
    return tq, tk, 32


def _attention(z3, slopes, lq1, lk1, lq2, lk2, gsub, lam_init):
    B, S, _ = z3.shape
    H = ATTN_HEADS
    tq, tk, rb = _attn_tiles(S)
    assert S % tk == 0 and (S // tk) % 2 == 0 and S // tk >= 4 and tk % tq == 0 and tq % rb == 0
    kern = functools.partial(_attn_kernel, tq=tq, tk=tk, S=S, rb=rb)
    vec = lambda n: pl.BlockSpec((1, n), lambda b, h: (0, 0))
    pair = lambda shape, dt: [pltpu.VMEM(shape, dt) for _ in range(2)]
    return pl.pallas_call(
        kern,
        grid=(B, H),
        in_specs=[
            pl.BlockSpec(memory_space=pltpu.SMEM),
            pl.BlockSpec(memory_space=pltpu.SMEM),
            pl.BlockSpec((1, S, HEAD_WIDTH), lambda b, h: (b, 0, h)),
            pl.BlockSpec((1, S, HEAD_WIDTH), lambda b, h: (b, 0, H + h), pipeline_mode=pl.Buffered(1)),
            pl.BlockSpec((1, S, HEAD_WIDTH), lambda b, h: (b, 0, 2 * H + h), pipeline_mode=pl.Buffered(1)),
            vec(ATTN_HEAD_DIM), vec(ATTN_HEAD_DIM), vec(ATTN_HEAD_DIM), vec(ATTN_HEAD_DIM),
            vec(HEAD_WIDTH),
        ],
        out_specs=pl.BlockSpec((1, S, HEAD_WIDTH), lambda b, h: (b, 0, h)),
        out_shape=jax.ShapeDtypeStruct((B, S, H * HEAD_WIDTH), BF16),
        scratch_shapes=[
            pltpu.VMEM((S, LANES), BF16),
            pltpu.VMEM((S, LANES), BF16),
            pltpu.VMEM((S, 2 * LANES), BF16),
            pltpu.VMEM((3, 2, tq, LANES), BF16),
            pair((tq, LANES), F32),
            pair((tq, 2 * LANES), F32),
            [pair((tq, tk), F32) for _ in range(2)],
            [pair((tq, tk), BF16) for _ in range(2)],
            [pair((tq, LANES), F32) for _ in range(2)],
        ],
        compiler_params=_cparams(("arbitrary", "arbitrary")),
        name="diff_attention",
    )(slopes, jnp.full((1,), lam_init, F32), z3, z3, z3, lq1, lk1, lq2, lk2, gsub)


def _post_kernel(o_ref, u_ref, vg_ref, ga_ref, gb_ref, x_ref, gt1_ref, sh2_ref, sc2_ref,
                 lng_ref, lnb_ref, ws_ref, bs_ref, wa_ref, wb_ref, wo_ref, gpost_ref,
                 gpre_ref, wr_ref, br_ref,
                 xo_ref, h2_ref, aff_ref, mixed_scr, *, tm):
    ya = jnp.dot(o_ref[...], wa_ref[...], preferred_element_type=F32)

    u = _gelu(u_ref[...].astype(F32))
    vg = _gelu(vg_ref[...].astype(F32))
    mu = jnp.mean(vg, axis=-1, keepdims=True)
    vc = vg - mu
    var = jnp.mean(vc * vc, axis=-1, keepdims=True)
    vgn = (vc * lax.rsqrt(var + NORM_EPS) * lng_ref[...] + lnb_ref[...]).astype(BF16)
    for c in range(tm // GMLP_CHUNK):
        rows = slice(c * GMLP_CHUNK, (c + 1) * GMLP_CHUNK)
        for g in range(GMLP_GROUPS):
            cols = slice(g * LANES, (g + 1) * LANES)
            blk = jnp.dot(ws_ref[g], vgn[rows, cols], preferred_element_type=F32)
            mixed_scr[rows, cols] = blk + bs_ref[:, g:g + 1]
    yb = jnp.dot((u * mixed_scr[...]).astype(BF16), wb_ref[...], preferred_element_type=F32)

    merged = (jax.nn.sigmoid(ga_ref[...].astype(F32)) * ya
              + jax.nn.sigmoid(gb_ref[...].astype(F32)) * yb)
    y = jnp.dot(merged.astype(BF16), wo_ref[...], preferred_element_type=F32)
    x = x_ref[...] + gt1_ref[0] * _rms(y, gpost_ref[...])
    xo_ref[...] = x

    h2 = _rms(x, gpre_ref[...]) * (1.0 + sc2_ref[0]) + sh2_ref[0]
    h2_ref[...] = h2
    logits = lax.dot_general(wr_ref[...], h2.astype(BF16), (((1,), (1,)), ((), ())),
                             preferred_element_type=F32) + br_ref[...]
    e = jnp.exp(logits - jnp.max(logits, axis=0, keepdims=True))
    aff_ref[0] = e / jnp.sum(e, axis=0, keepdims=True)


def _post(o2, z, x2, mod3, lng, lnb, ws, bsT, wa, wb, wo, gpost, gpre, wrT, br, S):
    T, D = x2.shape
    E = wrT.shape[0]
    tm = min(512, S)
    per = S // tm
    W = D
    kern = functools.partial(_post_kernel, tm=tm)
    zcol = lambda k: pl.BlockSpec((tm, W), lambda i: (i, k))
    modc = lambda k: pl.BlockSpec((1, 1, D), lambda i: (i // per, 0, k))
    full = lambda a: pl.BlockSpec(a.shape, lambda i: (0,) * a.ndim)
    return pl.pallas_call(
        kern,
        grid=(T // tm,),
        in_specs=[
            pl.BlockSpec((tm, W), lambda i: (i, 0)),
            zcol(3), zcol(4), zcol(5), zcol(6),
            pl.BlockSpec((tm, D), lambda i: (i, 0)),
            modc(2), modc(3), modc(4),
            full(lng), full(lnb), full(ws), full(bsT), full(wa), full(wb), full(wo),
            full(gpost), full(gpre), full(wrT), full(br),
        ],
        out_specs=[
            pl.BlockSpec((tm, D), lambda i: (i, 0)),
            pl.BlockSpec((tm, D), lambda i: (i, 0)),
            pl.BlockSpec((1, E, tm), lambda i: (i // per, 0, i % per)),
        ],
        out_shape=[
            jax.ShapeDtypeStruct((T, D), F32),
            jax.ShapeDtypeStruct((T, D), F32),
            jax.ShapeDtypeStruct((T // S, E, S), F32),
        ],
        scratch_shapes=[pltpu.VMEM((tm, W), F32)],
        compiler_params=_cparams(("arbitrary",)),
        name="mixer_tail",
    )(o2, z, z, z, z, x2, mod3, mod3, mod3, lng, lnb, ws, bsT, wa, wb, wo, gpost, gpre, wrT, br)


def _route_kernel(aff_ref, pos_ref, sg_ref, tok_ref, *, C):
    aff = aff_ref[0]
    E, nc, _ = aff.shape
    rows = E * nc
    bits = pltpu.bitcast(aff.reshape(rows, LANES), I32).reshape(E, nc, LANES)

    def count(mask):
        return jnp.sum(jnp.sum(mask.astype(I32), axis=2, keepdims=True), axis=1, keepdims=True)

    def step(it, thr):
        cand = thr | lax.shift_left(jnp.int32(1), 30 - it)
        return jnp.where(count(bits >= cand) >= C, cand, thr)

    thr = lax.fori_loop(0, 31, step, jnp.zeros((E, 1, 1), I32))
    gt = bits > thr
    tie = bits == thr
    need = (C - count(gt)).astype(F32)

    r = lax.broadcasted_iota(I32, (LANES, LANES), 0)
    c = lax.broadcasted_iota(I32, (LANES, LANES), 1)
    tri = jnp.where(r <= c, 1.0, 0.0).astype(BF16)
    row_id = (lax.broadcasted_iota(I32, (E, nc, rows), 0) * nc
              + lax.broadcasted_iota(I32, (E, nc, rows), 1)).reshape(rows, rows)
    first = (lax.broadcasted_iota(I32, (E, nc, rows), 0) * nc).reshape(rows, rows)
    col_id = lax.broadcasted_iota(I32, (rows, rows), 1)
    before = jnp.where((col_id < row_id) & (col_id >= first), 1.0, 0.0).astype(BF16)

    def seq_cumsum(x):
        within = jnp.dot(x.reshape(rows, LANES).astype(BF16), tri, preferred_element_type=F32)
        totals = jnp.broadcast_to(within[:, LANES - 1:LANES], (rows, LANES)).astype(BF16)
        return within, jnp.dot(before, totals, preferred_element_type=F32)

    tw, te = seq_cumsum(tie.astype(F32))
    sel = gt | (tie & ((tw + te).reshape(E, nc, LANES) <= need))
    self32 = sel.astype(F32)
    within, earlier = seq_cumsum(self32)
    pos_ref[0] = (within + earlier).reshape(E, nc, LANES).astype(I32) - sel.astype(I32)
    sg_ref[0] = jnp.where(sel, aff, 0.0)

    slot = lax.broadcasted_iota(I32, (1, C), 1).astype(F32)
    chunk_id = lax.broadcasted_iota(I32, (nc, C), 0).astype(F32)
    for e in range(E):
        w_e = within[e * nc:(e + 1) * nc, :]
        ex_e = earlier[e * nc:(e + 1) * nc, 0:1]
        upto = ex_e + w_e[:, LANES - 1:LANES]
        ch = jnp.sum(jnp.where(upto <= slot, 1.0, 0.0), axis=0, keepdims=True)
        onehot = chunk_id == ch
        local = slot - jnp.sum(jnp.where(onehot, ex_e, 0.0), axis=0, keepdims=True)
        counts = lax.dot_general(w_e.astype(BF16), jnp.where(onehot, 1.0, 0.0).astype(BF16),
                                 (((0,), (0,)), ((), ())), preferred_element_type=F32)
        place = jnp.sum(jnp.where(counts <= local, 1.0, 0.0), axis=0, keepdims=True)
        tok_ref[0, e:e + 1, :] = (ch * LANES + place).astype(I32)


def _route(aff, C):
    B, E, S = aff.shape
    nc = S // LANES
    kern = functools.partial(_route_kernel, C=C)
    blk = pl.BlockSpec((1, E, nc, LANES), lambda b: (b, 0, 0, 0))
    pos, sg, tok = pl.pallas_call(
        kern,
        grid=(B,),
        in_specs=[blk],
        out_specs=[blk, blk, pl.BlockSpec((1, E, C), lambda b: (b, 0, 0))],
        out_shape=[
            jax.ShapeDtypeStruct((B, E, nc, LANES), I32),
            jax.ShapeDtypeStruct((B, E, nc, LANES), F32),
            jax.ShapeDtypeStruct((B, E, C), I32),
        ],
        compiler_params=_cparams(("arbitrary",)),
        name="route",
    )(aff.reshape(B, E, nc, LANES))
    return pos.reshape(B, E, S), sg.reshape(B, E, S), tok.reshape(B * E, 1, C)


def _expert_kernel(tok_ref, tokn_ref, h2_hbm, wg_ref, wu_ref, wd_ref, y_ref, xbuf, xg_scr, acc_scr, sem,
                   *, S, C, E, R, nf):
    r = pl.program_id(0)
    f = pl.program_id(1)
    per = C // nf

    def row_copy(tok_smem, pair, s):
        t = tok_smem[0, 0, s]
        return pltpu.make_async_copy(h2_hbm.at[pl.ds((pair // E) * S + t, 1), :],
                                     xbuf.at[pl.ds(s, 1), :], sem)

    def wait_all_rows():
        pltpu.make_async_copy(h2_hbm.at[pl.ds(0, C), :], xbuf, sem).wait()

    @pl.when((r == 0) & (f == 0))
    def _():
        def issue(s, carry):
            row_copy(tok_ref, r, s).start()
            return carry

        lax.fori_loop(0, C, issue, 0)

    @pl.when(f == 0)
    def _():
        wait_all_rows()
        xg_scr[...] = xbuf[...].astype(BF16)
        acc_scr[...] = jnp.zeros_like(acc_scr)

    nxt = jnp.minimum(r + 1, R - 1)

    xg = xg_scr[...]
    g = jnp.dot(xg, wg_ref[0, 0].astype(BF16), preferred_element_type=F32)
    u = jnp.dot(xg, wu_ref[0, 0].astype(BF16), preferred_element_type=F32)
    hid = (g * jax.nn.sigmoid(g) * u).astype(BF16)
    acc_scr[...] += jnp.dot(hid, wd_ref[0, 0].astype(BF16), preferred_element_type=F32)
    for s in range(per):
        row_copy(tokn_ref, nxt, f * per + s).start()

    @pl.when(f == nf - 1)
    def _():
        y_ref[0] = acc_scr[...].astype(BF16)

    @pl.when((r == R - 1) & (f == nf - 1))
    def _():
        wait_all_rows()


def _experts(tok, h2, w_g, w_u, w_d, layer, B, S, C):
    _, E, D, F = w_g.shape
    tf = 512
    nf = F // tf
    R = B * E
    kern = functools.partial(_expert_kernel, S=S, C=C, E=E, R=R, nf=nf)
    return pl.pallas_call(
        kern,
        grid=(R, nf),
        in_specs=[
            pl.BlockSpec((1, 1, C), lambda r, f: (r, 0, 0), memory_space=pltpu.SMEM),
            pl.BlockSpec((1, 1, C), lambda r, f: (jnp.minimum(r + 1, R - 1), 0, 0),
                         memory_space=pltpu.SMEM),
            pl.BlockSpec(memory_space=pl.ANY),
            pl.BlockSpec((1, 1, D, tf), lambda r, f: (layer, r % E, 0, f)),
            pl.BlockSpec((1, 1, D, tf), lambda r, f: (layer, r % E, 0, f)),
            pl.BlockSpec((1, 1, tf, D), lambda r, f: (layer, r % E, f, 0)),
        ],
        out_specs=pl.BlockSpec((1, C, D), lambda r, f: (r, 0, 0)),
        out_shape=jax.ShapeDtypeStruct((R, C, D), BF16),
        scratch_shapes=[
            pltpu.VMEM((C, D), F32),
            pltpu.VMEM((C, D), BF16),
            pltpu.VMEM((C, D), F32),
            pltpu.SemaphoreType.DMA(()),
        ],
        compiler_params=_cparams(("arbitrary", "arbitrary")),
        name="experts",
    )(tok, tok, h2, w_g, w_u, w_d)


def _combine_kernel(starts_ref, y_hbm, pos_ref, sg_ref, x_ref, gt2_ref, gpost_ref, xo_ref,
                    win, sem, *, E, C, tt, W, nt, n_steps):
    b = pl.program_id(0)
    i = pl.program_id(1)
    step = b * nt + i

    par = lax.rem(step, 2)

    def copy(bb, ii, slot, e):
        s0 = starts_ref[(bb * E + e) * nt + ii]
        aligned = lax.shift_left(lax.shift_right_logical(s0, SUBLANE_SHIFT), SUBLANE_SHIFT)
        s0a = pl.multiple_of(jnp.minimum(aligned, C - W), BF16_SUBLANES)
        return s0a, pltpu.make_async_copy(y_hbm.at[bb * E + e, pl.ds(s0a, W), :], win.at[slot, e],
                                          sem.at[slot, e])

    @pl.when(step == 0)
    def _():
        for e in range(E):
            copy(b, i, par, e)[1].start()

    @pl.when(step + 1 < n_steps)
    def _():
        wrap = i + 1 == nt
        for e in range(E):
            copy(jnp.where(wrap, b + 1, b), jnp.where(wrap, 0, i + 1), 1 - par, e)[1].start()

    starts = []
    for e in range(E):
        s0a, cp = copy(b, i, par, e)
        cp.wait()
        starts.append(s0a)

    acc = jnp.zeros(xo_ref.shape, F32)
    for e in range(E):
        s0a = starts[e]
        slot = s0a + lax.broadcasted_iota(I32, (W, tt), 0)
        wt = jnp.where(pos_ref[0, e:e + 1, :] == slot, sg_ref[0, e:e + 1, :], 0.0).astype(BF16)
        acc = acc + lax.dot_general(wt, win[par, e], (((0,), (0,)), ((), ())),
                                    preferred_element_type=F32)

    xo_ref[...] = x_ref[...] + gt2_ref[0] * _rms(acc, gpost_ref[...])


def _combine_tile(C):
    return min(128, C // 2)


def _combine(starts, y, pos, sg, x2, mod3, gpost, S, C):
    T, D = x2.shape
    B, E, _ = pos.shape
    tt = _combine_tile(C)
    W = tt + BF16_SUBLANES
    nt = S // tt
    kern = functools.partial(_combine_kernel, E=E, C=C, tt=tt, W=W, nt=nt, n_steps=B * nt)
    return pl.pallas_call(
        kern,
        grid_spec=pltpu.PrefetchScalarGridSpec(
            num_scalar_prefetch=1,
            grid=(B, nt),
            in_specs=[
                pl.BlockSpec(memory_space=pl.ANY),
                pl.BlockSpec((1, E, tt), lambda b, i, st: (b, 0, i)),
                pl.BlockSpec((1, E, tt), lambda b, i, st: (b, 0, i)),
                pl.BlockSpec((tt, D), lambda b, i, st: (b * nt + i, 0)),
                pl.BlockSpec((1, 1, D), lambda b, i, st: (b, 0, 5)),
                pl.BlockSpec((1, D), lambda b, i, st: (0, 0)),
            ],
            out_specs=pl.BlockSpec((tt, D), lambda b, i, st: (b * nt + i, 0)),
            scratch_shapes=[
                pltpu.VMEM((2, E, W, D), BF16),
                pltpu.SemaphoreType.DMA((2, E)),
            ],
        ),
        out_shape=jax.ShapeDtypeStruct((T, D), F32),
        compiler_params=_cparams(("arbitrary", "arbitrary")),
        name="combine",
    )(starts, y, pos, sg, x2, mod3, gpost)


def _permute_qk(w):
    D = w.shape[0]
    return w.reshape(D, 2, ATTN_HEADS, ATTN_HEAD_DIM).transpose(0, 2, 1, 3).reshape(D, -1)


def kernel(x, c, w_ada, b_ada, g_pre_mix, g_post_mix, w_in, lam_q1, lam_k1, lam_q2, lam_k2, g_subln, ln_v_g, ln_v_b, w_spatial, b_spatial, w_branch_a, w_branch_b, w_out, g_pre_ffn, g_post_ffn, w_router, b_router, w_gate_e, w_up_e, w_down_e):
    B, S, D = x.shape
    L = w_ada.shape[0]
    E = w_router.shape[-1]
    C = EC_CAPACITY * S // E
    T = B * S
    qk = 2 * ATTN_HEADS * ATTN_HEAD_DIM

    mod = _modulation(c, w_ada, b_ada)
    slopes = 2.0 ** (-8.0 * jnp.arange(1, ATTN_HEADS + 1, dtype=F32) / ATTN_HEADS)
    x2 = x.reshape(T, D)
    row = lambda v: v.reshape(1, -1)

    for l in range(L):
        lam_init = 0.8 - 0.6 * math.exp(-0.3 * l)
        mod3 = mod[l].reshape(B, 1, 6 * D)
        w = w_in[l]
        w_l = jnp.concatenate(
            [_permute_qk(w[:, :qk]) * (LOG2E * ATTN_HEAD_DIM ** -0.5), _permute_qk(w[:, qk:2 * qk]),
             w[:, 2 * qk:]],
            axis=1).astype(BF16)

        z = _inproj(x2, mod3, row(g_pre_mix[l]), w_l, S)
        o = _attention(z.reshape(B, S, -1), slopes, row(lam_q1[l]), row(lam_k1[l]), row(lam_q2[l]),
                       row(lam_k2[l]), row(g_subln[l]), lam_init)
        x2, h2, aff = _post(
            o.reshape(T, -1), z, x2, mod3, row(ln_v_g[l]), row(ln_v_b[l]),
            w_spatial[l].astype(BF16), b_spatial[l].T, w_branch_a[l].astype(BF16),
            w_branch_b[l].astype(BF16), w_out[l].astype(BF16), row(g_post_mix[l]),
            row(g_pre_ffn[l]), w_router[l].T.astype(BF16), b_router[l].reshape(E, 1), S)

        pos, sg, tok = _route(aff, C)
        y = _experts(tok, h2, w_gate_e, w_up_e, w_down_e, l, B, S, C)
        starts = pos[:, :, ::_combine_tile(C)].reshape(-1)
        x2 = _combine(starts, y, pos, sg, x2, mod3, row(g_post_ffn[l]), S, C)

    return x2.reshape(B, S, D)
```

```python
import functools
import math

import numpy as np
import jax
import jax.numpy as jnp
from jax import lax
from jax.experimental import pallas as pl
from jax.experimental.pallas import tpu as pltpu

F32 = jnp.float32
BF16 = jnp.bfloat16
I32 = jnp.int32

NORM_EPS = 1e-6
ATTN_HEADS = 8
ATTN_HEAD_DIM = 64
HEAD_WIDTH = 2 * ATTN_HEAD_DIM
GMLP_CHUNK = 128
GMLP_GROUPS = 8
EC_CAPACITY = 2
LANES = 128
BF16_SUBLANES = 16
SUBLANE_SHIFT = 4
VMEM_LIMIT = 56 * 1024 * 1024
LOG2E = 1.4426950408889634
POS_SPLIT = 64
EXP2_ZERO = 160.0


def _bf16_split3(x):
    out = []
    r = np.float64(x)
    for _ in range(3):
        c = np.float64(np.asarray(r, np.float32).astype(BF16).astype(np.float32))
        out.append(float(c))
        r = r - c
    return out


LOG2E_BF16_PARTS = _bf16_split3(LOG2E)


def _cparams(sem):
    return pltpu.CompilerParams(dimension_semantics=sem, vmem_limit_bytes=VMEM_LIMIT)


def _rms(x, g):
    return x * lax.rsqrt(jnp.mean(x * x, axis=-1, keepdims=True) + NORM_EPS) * g


def _gelu(x):
    return 0.5 * x * (1.0 + lax.erf(x * (2.0 ** -0.5)))


def _mod_kernel(c_ref, w_ref, b_ref, o_ref):
    c = c_ref[...]
    ca = (c * jax.nn.sigmoid(c)).astype(BF16)
    o_ref[0] = jnp.dot(ca, w_ref[0].astype(BF16), preferred_element_type=F32) + b_ref[0]


def _modulation(c, w_ada, b_ada):
    L, D, W = w_ada.shape
    B = c.shape[0]
    return pl.pallas_call(
        _mod_kernel,
        grid=(L, W // D),
        in_specs=[
            pl.BlockSpec((B, D), lambda l, j: (0, 0)),
            pl.BlockSpec((1, D, D), lambda l, j: (l, 0, j)),
            pl.BlockSpec((1, 1, D), lambda l, j: (l, 0, j)),
        ],
        out_specs=pl.BlockSpec((1, B, D), lambda l, j: (l, 0, j)),
        out_shape=jax.ShapeDtypeStruct((L, B, W), F32),
        compiler_params=_cparams(("arbitrary", "arbitrary")),
        name="modulation",
    )(c, w_ada, b_ada.reshape(L, 1, W))


def _inproj_kernel(x_ref, sh_ref, sc_ref, g_ref, w_ref, z_ref, h_scr):
    @pl.when(pl.program_id(1) == 0)
    def _():
        h = _rms(x_ref[...], g_ref[...]) * (1.0 + sc_ref[0]) + sh_ref[0]
        h_scr[...] = h.astype(BF16)

    z_ref[...] = jnp.dot(h_scr[...], w_ref[...], preferred_element_type=F32).astype(BF16)


def _inproj(x2, mod3, g, w, S):
    T, D = x2.shape
    N = w.shape[1]
    tm = min(1024, S)
    tn = 1024
    per = S // tm
    return pl.pallas_call(
        _inproj_kernel,
        grid=(T // tm, N // tn),
        in_specs=[
            pl.BlockSpec((tm, D), lambda i, j: (i, 0)),
            pl.BlockSpec((1, 1, D), lambda i, j: (i // per, 0, 0)),
            pl.BlockSpec((1, 1, D), lambda i, j: (i // per, 0, 1)),
            pl.BlockSpec((1, D), lambda i, j: (0, 0)),
            pl.BlockSpec((D, tn), lambda i, j: (0, j)),
        ],
        out_specs=pl.BlockSpec((tm, tn), lambda i, j: (i, j)),
        out_shape=jax.ShapeDtypeStruct((T, N), BF16),
        scratch_shapes=[pltpu.VMEM((tm, D), BF16)],
        compiler_params=_cparams(("arbitrary", "arbitrary")),
        name="inproj",
    )(x2, mod3, mod3, g, w)


def _lane_tile(x, n):
    return jnp.concatenate([x] * n, axis=1)


def _bias_lanes(lane, base, v_hi, v_lo):
    return jnp.where((lane >= base) & (lane < base + 3), v_hi,
                     jnp.where((lane >= base + 3) & (lane < base + 6), v_lo, 0.0))


def _const_lanes(lane, base, sign):
    c = jnp.zeros(lane.shape, F32)
    for t, part in enumerate(LOG2E_BF16_PARTS):
        c = jnp.where((lane == base + t) | (lane == base + 3 + t), sign * part, c)
    return c


def _attn_kernel(head_ref, slope_ref, lam0_ref, q_ref, k_ref, v_ref, lq1_ref, lk1_ref, lq2_ref, lk2_ref,
                 gsub_ref, o_ref, ka_scr, kb_scr, va_scr, qaug_scr, m_scr, acc_scr, s_scr, p_scr, al_scr,
                 *, tq, tk, S, rb, nw):
    slope = slope_ref[head_ref[0]]
    slope2 = slope * LOG2E
    nk = S // tk
    nq = S // tq
    D = ATTN_HEAD_DIM
    dn = (((1,), (1,)), ((), ()))
    kx = (ka_scr, kb_scr)

    pc = min(512, S)
    for c in range(S // pc):
        rows = slice(c * pc, (c + 1) * pc)
        lane = lax.broadcasted_iota(I32, (pc, LANES), 1)
        pos = c * pc + lax.broadcasted_iota(I32, (pc, LANES), 0)
        hi = ((pos // POS_SPLIT) * POS_SPLIT).astype(F32) * slope
        lo = (pos % POS_SPLIT).astype(F32) * slope
        kk = k_ref[0, rows, :].astype(F32)
        ka_scr[rows, :] = jnp.where(lane < D, kk, _bias_lanes(lane, D, hi, lo)).astype(BF16)
        kb_scr[rows, :] = jnp.where(lane >= D, kk, _bias_lanes(lane, 0, hi, lo)).astype(BF16)
        va_scr[rows, :HEAD_WIDTH] = v_ref[0, rows, :]
        va_scr[rows, HEAD_WIDTH:] = jnp.where(lane == 0, 1.0, 0.0).astype(BF16)

    lam_init = lam0_ref[0]
    lam = (jnp.exp(jnp.sum(lq1_ref[...] * lk1_ref[...], axis=1, keepdims=True))
           - jnp.exp(jnp.sum(lq2_ref[...] * lk2_ref[...], axis=1, keepdims=True)) + lam_init)

    def tile_rows(qt):
        return pl.ds(pl.multiple_of(qt * tq, tq), tq)

    def first_chunk(qt):
        return lax.div(qt, tk // tq)

    def chunk_of(qt, t):
        jd = first_chunk(qt)
        start = jnp.clip(jd - _window_reach(nw), 0, nk - nw)
        rel = jd - start + t
        return start + jnp.where(rel >= nw, rel - nw, rel)

    def variant_of(qt, t):
        return jnp.where(chunk_of(qt, t) > first_chunk(qt), 1, 0)

    def build_q(qt):
        q = q_ref[0, tile_rows(qt), :].astype(F32)
        lane = lax.broadcasted_iota(I32, (tq, LANES), 1)
        for v, sign in enumerate((1.0, -1.0, 0.0)):
            qaug_scr[v, 0] = jnp.where(lane < D, q, _const_lanes(lane, D, sign)).astype(BF16)
            qaug_scr[v, 1] = jnp.where(lane >= D, q, _const_lanes(lane, 0, sign)).astype(BF16)

    def qk(qt, t, variant):
        ks = pl.ds(pl.multiple_of(chunk_of(qt, t) * tk, tk), tk)
        for mp in range(2):
            s_scr[t % 2][mp][...] = lax.dot_general(qaug_scr[variant, mp], kx[mp][ks, :], dn,
                                                    preferred_element_type=F32)

    def softmax(qt, t):
        par = t % 2
        c = chunk_of(qt, t)
        osign = jnp.where(c > first_chunk(qt), 1.0, -1.0).astype(F32)
        row0 = qt * tq
        for mp in range(2):
            for r in range(0, tq, rb):
                s = s_scr[par][mp][r:r + rb, :]
                if t == 0:
                    kpos = (c * tk + lax.broadcasted_iota(I32, (1, tk), 1)).astype(F32)
                    qpos = (row0 + r + lax.broadcasted_iota(I32, (rb, 1), 0)).astype(F32)
                    s = s - slope2 * jnp.abs(qpos - kpos)
                    m_new = jnp.broadcast_to(jnp.max(s, axis=1, keepdims=True), (rb, LANES))
                    shift = m_new
                    al_scr[par][mp][r:r + rb, :] = jnp.zeros((rb, LANES), F32)
                else:
                    qposr = (row0 + r + lax.broadcasted_iota(I32, (rb, LANES), 0)).astype(F32)
                    off = (osign * slope2) * qposr
                    mx = jnp.broadcast_to(jnp.max(s, axis=1, keepdims=True), (rb, LANES))
                    m_old = m_scr[mp][r:r + rb, :]
                    m_new = jnp.maximum(m_old, mx + off)
                    shift = m_new - off
                    al_scr[par][mp][r:r + rb, :] = jnp.exp2(m_old - m_new)
                p = jnp.exp2(s - _lane_tile(shift, tk // LANES))
                p_scr[par][mp][r:r + rb, :] = p.astype(BF16)
                m_scr[mp][r:r + rb, :] = m_new

    def pv(qt, t):
        par = t % 2
        ks = pl.ds(pl.multiple_of(chunk_of(qt, t) * tk, tk), tk)
        for mp in range(2):
            upd = jnp.dot(p_scr[par][mp][...], va_scr[ks, :], preferred_element_type=F32)
            if t == 0:
                acc_scr[mp][...] = upd
            else:
                acc_scr[mp][...] = _lane_tile(al_scr[par][mp][...], 2) * acc_scr[mp][...] + upd

    def finalize(qt):
        a1 = acc_scr[0][...]
        a2 = acc_scr[1][...]
        o1 = a1[:, :HEAD_WIDTH] / a1[:, HEAD_WIDTH:HEAD_WIDTH + 1]
        o2 = a2[:, :HEAD_WIDTH] / a2[:, HEAD_WIDTH:HEAD_WIDTH + 1]
        od = o1 - lam * o2
        o_ref[0, tile_rows(qt), :] = (_rms(od, gsub_ref[...]) * (1.0 - lam_init)).astype(BF16)

    def tile(qt, has_prev):
        build_q(qt)
        for t in range(nw):
            qk(qt, t, 2 if t == 0 else variant_of(qt, t))
            if t >= 1:
                softmax(qt, t - 1)
            elif has_prev:
                softmax(qt - 1, nw - 1)
            if t >= 2:
                pv(qt, t - 2)
            elif has_prev:
                pv(qt - 1, nw - 2 + t)
                if t == 1:
                    finalize(qt - 1)

    tile(jnp.int32(0), False)

    def body(qt, carry):
        tile(qt, True)
        return carry

    lax.fori_loop(1, nq, body, 0)
    softmax(nq - 1, nw - 1)
    pv(nq - 1, nw - 2)
    pv(nq - 1, nw - 1)
    finalize(nq - 1)


def _window_reach(nw):
    return (nw - 2) // 2


def _attn_tiles(S):
    tk = min(1024, S // 4)
    tq = min(512, tk)
    return tq, tk, 32


def _attention_head(nw, head, z3, slopes, lam0, lq1, lk1, lq2, lk2, gsub):
    B, S, _ = z3.shape
    H = ATTN_HEADS
    tq, tk, rb = _attn_tiles(S)
    kern = functools.partial(_attn_kernel, tq=tq, tk=tk, S=S, rb=rb, nw=nw)
    vec = lambda n: pl.BlockSpec((1, n), lambda b, hd: (0, 0))
    pair = lambda shape, dt: [pltpu.VMEM(shape, dt) for _ in range(2)]
    grid_spec = pltpu.PrefetchScalarGridSpec(
        num_scalar_prefetch=1,
        grid=(B,),
        in_specs=[
            pl.BlockSpec(memory_space=pltpu.SMEM),
            pl.BlockSpec(memory_space=pltpu.SMEM),
            pl.BlockSpec((1, S, HEAD_WIDTH), lambda b, hd: (b, 0, hd[0])),
            pl.BlockSpec((1, S, HEAD_WIDTH), lambda b, hd: (b, 0, H + hd[0]), pipeline_mode=pl.Buffered(1)),
            pl.BlockSpec((1, S, HEAD_WIDTH), lambda b, hd: (b, 0, 2 * H + hd[0]),
                         pipeline_mode=pl.Buffered(1)),
            vec(ATTN_HEAD_DIM), vec(ATTN_HEAD_DIM), vec(ATTN_HEAD_DIM), vec(ATTN_HEAD_DIM),
            vec(HEAD_WIDTH),
        ],
        out_specs=pl.BlockSpec((1, S, HEAD_WIDTH), lambda b, hd: (b, 0, 0)),
        scratch_shapes=[
            pltpu.VMEM((S, LANES), BF16),
            pltpu.VMEM((S, LANES), BF16),
            pltpu.VMEM((S, 2 * LANES), BF16),
            pltpu.VMEM((3, 2, tq, LANES), BF16),
            pair((tq, LANES), F32),
            pair((tq, 2 * LANES), F32),
            [pair((tq, tk), F32) for _ in range(2)],
            [pair((tq, tk), BF16) for _ in range(2)],
            [pair((tq, LANES), F32) for _ in range(2)],
        ],
    )
    return pl.pallas_call(
        kern,
        grid_spec=grid_spec,
        out_shape=jax.ShapeDtypeStruct((B, S, HEAD_WIDTH), BF16),
        compiler_params=_cparams(("arbitrary",)),
        name="diff_attention",
    )(head, slopes, lam0, z3, z3, z3, lq1, lk1, lq2, lk2, gsub)


def _attention(z3, slopes, lq1, lk1, lq2, lk2, gsub, lam_init):
    B, S, _ = z3.shape
    H = ATTN_HEADS
    tq, tk, rb = _attn_tiles(S)
    nk = S // tk
    assert S % tk == 0 and nk % 2 == 0 and nk >= 4 and tk % tq == 0 and tq % rb == 0
    width = H * HEAD_WIDTH

    def max_norm(cols):
        v = cols.astype(F32).reshape(B, S, H, 2, ATTN_HEAD_DIM)
        return jnp.sqrt(jnp.max(jnp.sum(v * v, axis=-1), axis=(0, 1, 3)))

    spread = 2.0 * max_norm(z3[..., :width]) * max_norm(z3[..., width:2 * width]) + EXP2_ZERO
    windows = [nw for nw in (4, 6) if nw < nk] + [nk]
    lam0 = jnp.full((1,), lam_init, F32)
    outs = []
    for h in range(H):
        fits = [spread[h] <= slopes[h] * LOG2E * (_window_reach(nw) * tk + 1) for nw in windows[:-1]]
        choice = jnp.int32(len(fits))
        for j in reversed(range(len(fits))):
            choice = jnp.where(fits[j], j, choice)
        branches = [functools.partial(_attention_head, nw) for nw in windows]
        outs.append(lax.switch(choice, branches, jnp.full((1,), h, I32), z3, slopes, lam0,
                               lq1, lk1, lq2, lk2, gsub))
    return jnp.concatenate(outs, axis=-1)


def _post_kernel(o_ref, u_ref, vg_ref, ga_ref, gb_ref, x_ref, gt1_ref, sh2_ref, sc2_ref,
                 lng_ref, lnb_ref, ws_ref, bs_ref, wa_ref, wb_ref, wo_ref, gpost_ref,
                 gpre_ref, wr_ref, br_ref,
                 xo_ref, h2_ref, aff_ref, mixed_scr, *, tm):
    ya = jnp.dot(o_ref[...], wa_ref[...], preferred_element_type=F32)

    u = _gelu(u_ref[...].astype(F32))
    vg = _gelu(vg_ref[...].astype(F32))
    mu = jnp.mean(vg, axis=-1, keepdims=True)
    vc = vg - mu
    var = jnp.mean(vc * vc, axis=-1, keepdims=True)
    vgn = (vc * lax.rsqrt(var + NORM_EPS) * lng_ref[...] + lnb_ref[...]).astype(BF16)
    for c in range(tm // GMLP_CHUNK):
        rows = slice(c * GMLP_CHUNK, (c + 1) * GMLP_CHUNK)
        for g in range(GMLP_GROUPS):
            cols = slice(g * LANES, (g + 1) * LANES)
            blk = jnp.dot(ws_ref[g], vgn[rows, cols], preferred_element_type=F32)
            mixed_scr[rows, cols] = blk + bs_ref[:, g:g + 1]
    yb = jnp.dot((u * mixed_scr[...]).astype(BF16), wb_ref[...], preferred_element_type=F32)

    merged = (jax.nn.sigmoid(ga_ref[...].astype(F32)) * ya
              + jax.nn.sigmoid(gb_ref[...].astype(F32)) * yb)
    y = jnp.dot(merged.astype(BF16), wo_ref[...], preferred_element_type=F32)
    x = x_ref[...] + gt1_ref[0] * _rms(y, gpost_ref[...])
    xo_ref[...] = x

    h2 = _rms(x, gpre_ref[...]) * (1.0 + sc2_ref[0]) + sh2_ref[0]
    h2_ref[...] = h2
    logits = lax.dot_general(wr_ref[...], h2.astype(BF16), (((1,), (1,)), ((), ())),
                             preferred_element_type=F32) + br_ref[...]
    e = jnp.exp(logits - jnp.max(logits, axis=0, keepdims=True))
    aff_ref[0] = e / jnp.sum(e, axis=0, keepdims=True)


def _post(o2, z, x2, mod3, lng, lnb, ws, bsT, wa, wb, wo, gpost, gpre, wrT, br, S):
    T, D = x2.shape
    E = wrT.shape[0]
    tm = min(512, S)
    per = S // tm
    W = D
    kern = functools.partial(_post_kernel, tm=tm)
    zcol = lambda k: pl.BlockSpec((tm, W), lambda i: (i, k))
    modc = lambda k: pl.BlockSpec((1, 1, D), lambda i: (i // per, 0, k))
    full = lambda a: pl.BlockSpec(a.shape, lambda i: (0,) * a.ndim)
    return pl.pallas_call(
        kern,
        grid=(T // tm,),
        in_specs=[
            pl.BlockSpec((tm, W), lambda i: (i, 0)),
            zcol(3), zcol(4), zcol(5), zcol(6),
            pl.BlockSpec((tm, D), lambda i: (i, 0)),
            modc(2), modc(3), modc(4),
            full(lng), full(lnb), full(ws), full(bsT), full(wa), full(wb), full(wo),
            full(gpost), full(gpre), full(wrT), full(br),
        ],
        out_specs=[
            pl.BlockSpec((tm, D), lambda i: (i, 0)),
            pl.BlockSpec((tm, D), lambda i: (i, 0)),
            pl.BlockSpec((1, E, tm), lambda i: (i // per, 0, i % per)),
        ],
        out_shape=[
            jax.ShapeDtypeStruct((T, D), F32),
            jax.ShapeDtypeStruct((T, D), F32),
            jax.ShapeDtypeStruct((T // S, E, S), F32),
        ],
        scratch_shapes=[pltpu.VMEM((tm, W), F32)],
        compiler_params=_cparams(("arbitrary",)),
        name="mixer_tail",
    )(o2, z, z, z, z, x2, mod3, mod3, mod3, lng, lnb, ws, bsT, wa, wb, wo, gpost, gpre, wrT, br)


def _route_kernel(aff_ref, pos_ref, sg_ref, tok_ref, *, C):
    aff = aff_ref[0]
    E, nc, _ = aff.shape
    rows = E * nc
    bits = pltpu.bitcast(aff.reshape(rows, LANES), I32).reshape(E, nc, LANES)

    def count(mask):
        return jnp.sum(jnp.sum(mask.astype(I32), axis=2, keepdims=True), axis=1, keepdims=True)

    def step(it, thr):
        cand = thr | lax.shift_left(jnp.int32(1), 30 - it)
        return jnp.where(count(bits >= cand) >= C, cand, thr)

    thr = lax.fori_loop(0, 31, step, jnp.zeros((E, 1, 1), I32))
    gt = bits > thr
    tie = bits == thr
    need = (C - count(gt)).astype(F32)

    r = lax.broadcasted_iota(I32, (LANES, LANES), 0)
    c = lax.broadcasted_iota(I32, (LANES, LANES), 1)
    tri = jnp.where(r <= c, 1.0, 0.0).astype(BF16)
    row_id = (lax.broadcasted_iota(I32, (E, nc, rows), 0) * nc
              + lax.broadcasted_iota(I32, (E, nc, rows), 1)).reshape(rows, rows)
    first = (lax.broadcasted_iota(I32, (E, nc, rows), 0) * nc).reshape(rows, rows)
    col_id = lax.broadcasted_iota(I32, (rows, rows), 1)
    before = jnp.where((col_id < row_id) & (col_id >= first), 1.0, 0.0).astype(BF16)

    def seq_cumsum(x):
        within = jnp.dot(x.reshape(rows, LANES).astype(BF16), tri, preferred_element_type=F32)
        totals = jnp.broadcast_to(within[:, LANES - 1:LANES], (rows, LANES)).astype(BF16)
        return within, jnp.dot(before, totals, preferred_element_type=F32)

    tw, te = seq_cumsum(tie.astype(F32))
    sel = gt | (tie & ((tw + te).reshape(E, nc, LANES) <= need))
    self32 = sel.astype(F32)
    within, earlier = seq_cumsum(self32)
    pos_ref[0] = (within + earlier).reshape(E, nc, LANES).astype(I32) - sel.astype(I32)
    sg_ref[0] = jnp.where(sel, aff, 0.0)

    slot = lax.broadcasted_iota(I32, (1, C), 1).astype(F32)
    chunk_id = lax.broadcasted_iota(I32, (nc, C), 0).astype(F32)
    for e in range(E):
        w_e = within[e * nc:(e + 1) * nc, :]
        ex_e = earlier[e * nc:(e + 1) * nc, 0:1]
        upto = ex_e + w_e[:, LANES - 1:LANES]
        ch = jnp.sum(jnp.where(upto <= slot, 1.0, 0.0), axis=0, keepdims=True)
        onehot = chunk_id == ch
        local = slot - jnp.sum(jnp.where(onehot, ex_e, 0.0), axis=0, keepdims=True)
        counts = lax.dot_general(w_e.astype(BF16), jnp.where(onehot, 1.0, 0.0).astype(BF16),
                                 (((0,), (0,)), ((), ())), preferred_element_type=F32)
        place = jnp.sum(jnp.where(counts <= local, 1.0, 0.0), axis=0, keepdims=True)
        tok_ref[0, e:e + 1, :] = (ch * LANES + place).astype(I32)


def _route(aff, C):
    B, E, S = aff.shape
    nc = S // LANES
    kern = functools.partial(_route_kernel, C=C)
    blk = pl.BlockSpec((1, E, nc, LANES), lambda b: (b, 0, 0, 0))
    pos, sg, tok = pl.pallas_call(
        kern,
        grid=(B,),
        in_specs=[blk],
        out_specs=[blk, blk, pl.BlockSpec((1, E, C), lambda b: (b, 0, 0))],
        out_shape=[
            jax.ShapeDtypeStruct((B, E, nc, LANES), I32),
            jax.ShapeDtypeStruct((B, E, nc, LANES), F32),
            jax.ShapeDtypeStruct((B, E, C), I32),
        ],
        compiler_params=_cparams(("arbitrary",)),
        name="route",
    )(aff.reshape(B, E, nc, LANES))
    return pos.reshape(B, E, S), sg.reshape(B, E, S), tok.reshape(B * E, 1, C)


def _expert_kernel(tok_ref, tokn_ref, h2_hbm, wg_ref, wu_ref, wd_ref, y_ref, xbuf, xg_scr, acc_scr, sem,
                   *, S, C, E, R, nf):
    r = pl.program_id(0)
    f = pl.program_id(1)
    per = C // nf

    def row_copy(tok_smem, pair, s):
        t = tok_smem[0, 0, s]
        return pltpu.make_async_copy(h2_hbm.at[pl.ds((pair // E) * S + t, 1), :],
                                     xbuf.at[pl.ds(s, 1), :], sem)

    def wait_all_rows():
        pltpu.make_async_copy(h2_hbm.at[pl.ds(0, C), :], xbuf, sem).wait()

    @pl.when((r == 0) & (f == 0))
    def _():
        def issue(s, carry):
            row_copy(tok_ref, r, s).start()
            return carry

        lax.fori_loop(0, C, issue, 0)

    @pl.when(f == 0)
    def _():
        wait_all_rows()
        xg_scr[...] = xbuf[...].astype(BF16)
        acc_scr[...] = jnp.zeros_like(acc_scr)

    nxt = jnp.minimum(r + 1, R - 1)

    xg = xg_scr[...]
    g = jnp.dot(xg, wg_ref[0, 0].astype(BF16), preferred_element_type=F32)
    u = jnp.dot(xg, wu_ref[0, 0].astype(BF16), preferred_element_type=F32)
    hid = (g * jax.nn.sigmoid(g) * u).astype(BF16)
    acc_scr[...] += jnp.dot(hid, wd_ref[0, 0].astype(BF16), preferred_element_type=F32)
    for s in range(per):
        row_copy(tokn_ref, nxt, f * per + s).start()

    @pl.when(f == nf - 1)
    def _():
        y_ref[0] = acc_scr[...].astype(BF16)

    @pl.when((r == R - 1) & (f == nf - 1))
    def _():
        wait_all_rows()


def _experts(tok, h2, w_g, w_u, w_d, layer, B, S, C):
    _, E, D, F = w_g.shape
    tf = 512
    nf = F // tf
    R = B * E
    kern = functools.partial(_expert_kernel, S=S, C=C, E=E, R=R, nf=nf)
    return pl.pallas_call(
        kern,
        grid=(R, nf),
        in_specs=[
            pl.BlockSpec((1, 1, C), lambda r, f: (r, 0, 0), memory_space=pltpu.SMEM),
            pl.BlockSpec((1, 1, C), lambda r, f: (jnp.minimum(r + 1, R - 1), 0, 0),
                         memory_space=pltpu.SMEM),
            pl.BlockSpec(memory_space=pl.ANY),
            pl.BlockSpec((1, 1, D, tf), lambda r, f: (layer, r % E, 0, f)),
            pl.BlockSpec((1, 1, D, tf), lambda r, f: (layer, r % E, 0, f)),
            pl.BlockSpec((1, 1, tf, D), lambda r, f: (layer, r % E, f, 0)),
        ],
        out_specs=pl.BlockSpec((1, C, D), lambda r, f: (r, 0, 0)),
        out_shape=jax.ShapeDtypeStruct((R, C, D), BF16),
        scratch_shapes=[
            pltpu.VMEM((C, D), F32),
            pltpu.VMEM((C, D), BF16),
            pltpu.VMEM((C, D), F32),
            pltpu.SemaphoreType.DMA(()),
        ],
        compiler_params=_cparams(("arbitrary", "arbitrary")),
        name="experts",
    )(tok, tok, h2, w_g, w_u, w_d)


def _combine_kernel(starts_ref, y_hbm, pos_ref, sg_ref, x_ref, gt2_ref, gpost_ref, xo_ref,
                    win, sem, *, E, C, tt, W, nt, n_steps):
    b = pl.program_id(0)
    i = pl.program_id(1)
    step = b * nt + i

    par = lax.rem(step, 2)

    def copy(bb, ii, slot, e):
        s0 = starts_ref[(bb * E + e) * nt + ii]
        aligned = lax.shift_left(lax.shift_right_logical(s0, SUBLANE_SHIFT), SUBLANE_SHIFT)
        s0a = pl.multiple_of(jnp.minimum(aligned, C - W), BF16_SUBLANES)
        return s0a, pltpu.make_async_copy(y_hbm.at[bb * E + e, pl.ds(s0a, W), :], win.at[slot, e],
                                          sem.at[slot, e])

    @pl.when(step == 0)
    def _():
        for e in range(E):
            copy(b, i, par, e)[1].start()

    @pl.when(step + 1 < n_steps)
    def _():
        wrap = i + 1 == nt
        for e in range(E):
            copy(jnp.where(wrap, b + 1, b), jnp.where(wrap, 0, i + 1), 1 - par, e)[1].start()

    starts = []
    for e in range(E):
        s0a, cp = copy(b, i, par, e)
        cp.wait()
        starts.append(s0a)

    acc = jnp.zeros(xo_ref.shape, F32)
    for e in range(E):
        s0a = starts[e]
        slot = s0a + lax.broadcasted_iota(I32, (W, tt), 0)
        wt = jnp.where(pos_ref[0, e:e + 1, :] == slot, sg_ref[0, e:e + 1, :], 0.0).astype(BF16)
        acc = acc + lax.dot_general(wt, win[par, e], (((0,), (0,)), ((), ())),
                                    preferred_element_type=F32)

    xo_ref[...] = x_ref[...] + gt2_ref[0] * _rms(acc, gpost_ref[...])


def _combine_tile(C):
    return min(128, C // 2)


def _combine(starts, y, pos, sg, x2, mod3, gpost, S, C):
    T, D = x2.shape
    B, E, _ = pos.shape
    tt = _combine_tile(C)
    W = tt + BF16_SUBLANES
    nt = S // tt
    kern = functools.partial(_combine_kernel, E=E, C=C, tt=tt, W=W, nt=nt, n_steps=B * nt)
    return pl.pallas_call(
        kern,
        grid_spec=pltpu.PrefetchScalarGridSpec(
            num_scalar_prefetch=1,
            grid=(B, nt),
            in_specs=[
                pl.BlockSpec(memory_space=pl.ANY),
                pl.BlockSpec((1, E, tt), lambda b, i, st: (b, 0, i)),
                pl.BlockSpec((1, E, tt), lambda b, i, st: (b, 0, i)),
                pl.BlockSpec((tt, D), lambda b, i, st: (b * nt + i, 0)),
                pl.BlockSpec((1, 1, D), lambda b, i, st: (b, 0, 5)),
                pl.BlockSpec((1, D), lambda b, i, st: (0, 0)),
            ],
            out_specs=pl.BlockSpec((tt, D), lambda b, i, st: (b * nt + i, 0)),
            scratch_shapes=[
                pltpu.VMEM((2, E, W, D), BF16),
                pltpu.SemaphoreType.DMA((2, E)),
            ],
        ),
        out_shape=jax.ShapeDtypeStruct((T, D), F32),
        compiler_params=_cparams(("arbitrary", "arbitrary")),
        name="combine",
    )(starts, y, pos, sg, x2, mod3, gpost)


def _permute_qk(w):
    D = w.shape[0]
    return w.reshape(D, 2, ATTN_HEADS, ATTN_HEAD_DIM).transpose(0, 2, 1, 3).reshape(D, -1)


def kernel(x, c, w_ada, b_ada, g_pre_mix, g_post_mix, w_in, lam_q1, lam_k1, lam_q2, lam_k2, g_subln, ln_v_g, ln_v_b, w_spatial, b_spatial, w_branch_a, w_branch_b, w_out, g_pre_ffn, g_post_ffn, w_router, b_router, w_gate_e, w_up_e, w_down_e):
    B, S, D = x.shape
    L = w_ada.shape[0]
    E = w_router.shape[-1]
    C = EC_CAPACITY * S // E
    T = B * S
    qk = 2 * ATTN_HEADS * ATTN_HEAD_DIM

    mod = _modulation(c, w_ada, b_ada)
    slopes = 2.0 ** (-8.0 * jnp.arange(1, ATTN_HEADS + 1, dtype=F32) / ATTN_HEADS)
    x2 = x.reshape(T, D)
    row = lambda v: v.reshape(1, -1)

    for l in range(L):
        lam_init = 0.8 - 0.6 * math.exp(-0.3 * l)
        mod3 = mod[l].reshape(B, 1, 6 * D)
        w = w_in[l]
        w_l = jnp.concatenate(
            [_permute_qk(w[:, :qk]) * (LOG2E * ATTN_HEAD_DIM ** -0.5), _permute_qk(w[:, qk:2 * qk]),
             w[:, 2 * qk:]],
            axis=1).astype(BF16)

        z = _inproj(x2, mod3, row(g_pre_mix[l]), w_l, S)
        o = _attention(z.reshape(B, S, -1), slopes, row(lam_q1[l]), row(lam_k1[l]), row(lam_q2[l]),
                       row(lam_k2[l]), row(g_subln[l]), lam_init)
        x2, h2, aff = _post(
            o.reshape(T, -1), z, x2, mod3, row(ln_v_g[l]), row(ln_v_b[l]),
            w_spatial[l].astype(BF16), b_spatial[l].T, w_branch_a[l].astype(BF16),
            w_branch_b[l].astype(BF16), w_out[l].astype(BF16), row(g_post_mix[l]),
            row(g_pre_ffn[l]), w_router[l].T.astype(BF16), b_router[l].reshape(E, 1), S)

        pos, sg, tok = _route(aff, C)
        y = _experts(tok, h2, w_gate_e, w_up_e, w_down_e, l, B, S, C)
        starts = pos[:, :, ::_combine_tile(C)].reshape(-1)
        x2 = _combine(starts, y, pos, sg, x2, mod3, row(g_post_ffn[l]), S, C)

    return x2.reshape(B, S, D)
```

```python
import functools
import math

import numpy as np
import jax
import jax.numpy as jnp
from jax import lax
from jax.experimental import pallas as pl
from jax.experimental.pallas import tpu as pltpu

F32 = jnp.float32
BF16 = jnp.bfloat16
I32 = jnp.int32

NORM_EPS = 1e-6
ATTN_HEADS = 8
ATTN_HEAD_DIM = 64
HEAD_WIDTH = 2 * ATTN_HEAD_DIM
GMLP_CHUNK = 128
GMLP_GROUPS = 8
EC_CAPACITY = 2
LANES = 128
BF16_SUBLANES = 16
SUBLANE_SHIFT = 4
VMEM_LIMIT = 56 * 1024 * 1024
LOG2E = 1.4426950408889634
POS_SPLIT = 64
EXP2_ZERO = 160.0


def _bf16_split3(x):
    out = []
    r = np.float64(x)
    for _ in range(3):
        c = np.float64(np.asarray(r, np.float32).astype(BF16).astype(np.float32))
        out.append(float(c))
        r = r - c
    return out


LOG2E_BF16_PARTS = _bf16_split3(LOG2E)


def _cparams(sem):
    return pltpu.CompilerParams(dimension_semantics=sem, vmem_limit_bytes=VMEM_LIMIT)


def _rms(x, g):
    return x * lax.rsqrt(jnp.mean(x * x, axis=-1, keepdims=True) + NORM_EPS) * g


def _gelu(x):
    return 0.5 * x * (1.0 + lax.erf(x * (2.0 ** -0.5)))


def _mod_kernel(c_ref, w_ref, b_ref, o_ref):
    c = c_ref[...]
    ca = (c * jax.nn.sigmoid(c)).astype(BF16)
    o_ref[0] = jnp.dot(ca, w_ref[0].astype(BF16), preferred_element_type=F32) + b_ref[0]


def _modulation(c, w_ada, b_ada):
    L, D, W = w_ada.shape
    B = c.shape[0]
    return pl.pallas_call(
        _mod_kernel,
        grid=(L, W // D),
        in_specs=[
            pl.BlockSpec((B, D), lambda l, j: (0, 0)),
            pl.BlockSpec((1, D, D), lambda l, j: (l, 0, j)),
            pl.BlockSpec((1, 1, D), lambda l, j: (l, 0, j)),
        ],
        out_specs=pl.BlockSpec((1, B, D), lambda l, j: (l, 0, j)),
        out_shape=jax.ShapeDtypeStruct((L, B, W), F32),
        compiler_params=_cparams(("arbitrary", "arbitrary")),
        name="modulation",
    )(c, w_ada, b_ada.reshape(L, 1, W))


def _inproj_kernel(x_ref, sh_ref, sc_ref, g_ref, w_ref, z_ref, h_scr):
    @pl.when(pl.program_id(1) == 0)
    def _():
        h = _rms(x_ref[...], g_ref[...]) * (1.0 + sc_ref[0]) + sh_ref[0]
        h_scr[...] = h.astype(BF16)

    z_ref[...] = jnp.dot(h_scr[...], w_ref[...], preferred_element_type=F32).astype(BF16)


def _inproj(x2, mod3, g, w, S):
    T, D = x2.shape
    N = w.shape[1]
    tm = min(1024, S)
    tn = 1024
    per = S // tm
    return pl.pallas_call(
        _inproj_kernel,
        grid=(T // tm, N // tn),
        in_specs=[
            pl.BlockSpec((tm, D), lambda i, j: (i, 0)),
            pl.BlockSpec((1, 1, D), lambda i, j: (i // per, 0, 0)),
            pl.BlockSpec((1, 1, D), lambda i, j: (i // per, 0, 1)),
            pl.BlockSpec((1, D), lambda i, j: (0, 0)),
            pl.BlockSpec((D, tn), lambda i, j: (0, j)),
        ],
        out_specs=pl.BlockSpec((tm, tn), lambda i, j: (i, j)),
        out_shape=jax.ShapeDtypeStruct((T, N), BF16),
        scratch_shapes=[pltpu.VMEM((tm, D), BF16)],
        compiler_params=_cparams(("arbitrary", "arbitrary")),
        name="inproj",
    )(x2, mod3, mod3, g, w)


def _lane_tile(x, n):
    return jnp.concatenate([x] * n, axis=1)


def _bias_lanes(lane, base, v_hi, v_lo):
    return jnp.where((lane >= base) & (lane < base + 3), v_hi,
                     jnp.where((lane >= base + 3) & (lane < base + 6), v_lo, 0.0))


def _const_lanes(lane, base, sign):
    c = jnp.zeros(lane.shape, F32)
    for t, part in enumerate(LOG2E_BF16_PARTS):
        c = jnp.where((lane == base + t) | (lane == base + 3 + t), sign * part, c)
    return c


def _attn_kernel(choice_ref, slope_ref, lam0_ref, q_ref, k_ref, v_ref, lq1_ref, lk1_ref, lq2_ref, lk2_ref,
                 gsub_ref, o_ref, ka_scr, kb_scr, va_scr, qaug_scr, m_scr, acc_scr, s_scr, p_scr, al_scr,
                 *, tq, tk, S, rb, windows):
    head = pl.program_id(1)
    slope = slope_ref[head]
    slope2 = slope * LOG2E
    nk = S // tk
    nq = S // tq
    D = ATTN_HEAD_DIM
    dn = (((1,), (1,)), ((), ()))
    kx = (ka_scr, kb_scr)

    pc = min(512, S)
    for c in range(S // pc):
        rows = slice(c * pc, (c + 1) * pc)
        lane = lax.broadcasted_iota(I32, (pc, LANES), 1)
        pos = c * pc + lax.broadcasted_iota(I32, (pc, LANES), 0)
        hi = ((pos // POS_SPLIT) * POS_SPLIT).astype(F32) * slope
        lo = (pos % POS_SPLIT).astype(F32) * slope
        kk = k_ref[0, rows, :].astype(F32)
        ka_scr[rows, :] = jnp.where(lane < D, kk, _bias_lanes(lane, D, hi, lo)).astype(BF16)
        kb_scr[rows, :] = jnp.where(lane >= D, kk, _bias_lanes(lane, 0, hi, lo)).astype(BF16)
        va_scr[rows, :HEAD_WIDTH] = v_ref[0, rows, :]
        va_scr[rows, HEAD_WIDTH:] = jnp.where(lane == 0, 1.0, 0.0).astype(BF16)

    lam_init = lam0_ref[0]
    lam = (jnp.exp(jnp.sum(lq1_ref[...] * lk1_ref[...], axis=1, keepdims=True))
           - jnp.exp(jnp.sum(lq2_ref[...] * lk2_ref[...], axis=1, keepdims=True)) + lam_init)

    traced_window = [nk]

    def tile_rows(qt):
        return pl.ds(pl.multiple_of(qt * tq, tq), tq)

    def first_chunk(qt):
        return lax.div(qt, tk // tq)

    def chunk_of(qt, t):
        nw = traced_window[0]
        jd = first_chunk(qt)
        start = jnp.clip(jd - _window_reach(nw), 0, nk - nw)
        rel = jd - start + t
        return start + jnp.where(rel >= nw, rel - nw, rel)

    def variant_of(qt, t):
        return jnp.where(chunk_of(qt, t) > first_chunk(qt), 1, 0)

    def build_q(qt):
        q = q_ref[0, tile_rows(qt), :].astype(F32)
        lane = lax.broadcasted_iota(I32, (tq, LANES), 1)
        for v, sign in enumerate((1.0, -1.0, 0.0)):
            qaug_scr[v, 0] = jnp.where(lane < D, q, _const_lanes(lane, D, sign)).astype(BF16)
            qaug_scr[v, 1] = jnp.where(lane >= D, q, _const_lanes(lane, 0, sign)).astype(BF16)

    def qk(qt, t, variant):
        ks = pl.ds(pl.multiple_of(chunk_of(qt, t) * tk, tk), tk)
        for mp in range(2):
            s_scr[t % 2][mp][...] = lax.dot_general(qaug_scr[variant, mp], kx[mp][ks, :], dn,
                                                    preferred_element_type=F32)

    def softmax(qt, t):
        par = t % 2
        c = chunk_of(qt, t)
        osign = jnp.where(c > first_chunk(qt), 1.0, -1.0).astype(F32)
        row0 = qt * tq
        for mp in range(2):
            for r in range(0, tq, rb):
                s = s_scr[par][mp][r:r + rb, :]
                if t == 0:
                    kpos = (c * tk + lax.broadcasted_iota(I32, (1, tk), 1)).astype(F32)
                    qpos = (row0 + r + lax.broadcasted_iota(I32, (rb, 1), 0)).astype(F32)
                    s = s - slope2 * jnp.abs(qpos - kpos)
                    m_new = jnp.broadcast_to(jnp.max(s, axis=1, keepdims=True), (rb, LANES))
                    shift = m_new
                    al_scr[par][mp][r:r + rb, :] = jnp.zeros((rb, LANES), F32)
                else:
                    qposr = (row0 + r + lax.broadcasted_iota(I32, (rb, LANES), 0)).astype(F32)
                    off = (osign * slope2) * qposr
                    mx = jnp.broadcast_to(jnp.max(s, axis=1, keepdims=True), (rb, LANES))
                    m_old = m_scr[mp][r:r + rb, :]
                    m_new = jnp.maximum(m_old, mx + off)
                    shift = m_new - off
                    al_scr[par][mp][r:r + rb, :] = jnp.exp2(m_old - m_new)
                p = jnp.exp2(s - _lane_tile(shift, tk // LANES))
                p_scr[par][mp][r:r + rb, :] = p.astype(BF16)
                m_scr[mp][r:r + rb, :] = m_new

    def pv(qt, t):
        par = t % 2
        ks = pl.ds(pl.multiple_of(chunk_of(qt, t) * tk, tk), tk)
        for mp in range(2):
            upd = jnp.dot(p_scr[par][mp][...], va_scr[ks, :], preferred_element_type=F32)
            if t == 0:
                acc_scr[mp][...] = upd
            else:
                acc_scr[mp][...] = _lane_tile(al_scr[par][mp][...], 2) * acc_scr[mp][...] + upd

    def finalize(qt):
        a1 = acc_scr[0][...]
        a2 = acc_scr[1][...]
        o1 = a1[:, :HEAD_WIDTH] / a1[:, HEAD_WIDTH:HEAD_WIDTH + 1]
        o2 = a2[:, :HEAD_WIDTH] / a2[:, HEAD_WIDTH:HEAD_WIDTH + 1]
        od = o1 - lam * o2
        o_ref[0, tile_rows(qt), :] = (_rms(od, gsub_ref[...]) * (1.0 - lam_init)).astype(BF16)

    def tile(qt, has_prev):
        nw = traced_window[0]
        build_q(qt)
        for t in range(nw):
            qk(qt, t, 2 if t == 0 else variant_of(qt, t))
            if t >= 1:
                softmax(qt, t - 1)
            elif has_prev:
                softmax(qt - 1, nw - 1)
            if t >= 2:
                pv(qt, t - 2)
            elif has_prev:
                pv(qt - 1, nw - 2 + t)
                if t == 1:
                    finalize(qt - 1)

    def run(nw):
        traced_window[0] = nw
        tile(jnp.int32(0), False)

        def body(qt, carry):
            tile(qt, True)
            return carry

        lax.fori_loop(1, nq, body, 0)
        softmax(nq - 1, nw - 1)
        pv(nq - 1, nw - 2)
        pv(nq - 1, nw - 1)
        finalize(nq - 1)
        return jnp.int32(0)

    lax.switch(choice_ref[head], [functools.partial(run, nw) for nw in windows])


def _window_reach(nw):
    return (nw - 2) // 2


def _attn_tiles(S):
    tk = min(1024, S // 4)
    tq = min(512, tk)
    return tq, tk, 32


def _attention(z3, slopes, lq1, lk1, lq2, lk2, gsub, lam_init):
    B, S, _ = z3.shape
    H = ATTN_HEADS
    tq, tk, rb = _attn_tiles(S)
    nk = S // tk
    assert S % tk == 0 and nk % 2 == 0 and nk >= 4 and tk % tq == 0 and tq % rb == 0
    width = H * HEAD_WIDTH

    def max_norm(cols):
        v = cols.astype(F32).reshape(B, S, H, 2, ATTN_HEAD_DIM)
        return jnp.sqrt(jnp.max(jnp.sum(v * v, axis=-1), axis=(0, 1, 3)))

    spread = 2.0 * max_norm(z3[..., :width]) * max_norm(z3[..., width:2 * width]) + EXP2_ZERO
    windows = tuple(nw for nw in (4, 6) if nw < nk) + (nk,)
    choice = jnp.full((H,), len(windows) - 1, I32)
    for j in reversed(range(len(windows) - 1)):
        fits = spread <= slopes * LOG2E * (_window_reach(windows[j]) * tk + 1)
        choice = jnp.where(fits, j, choice)

    kern = functools.partial(_attn_kernel, tq=tq, tk=tk, S=S, rb=rb, windows=windows)
    vec = lambda n: pl.BlockSpec((1, n), lambda b, h, ch: (0, 0))
    pair = lambda shape, dt: [pltpu.VMEM(shape, dt) for _ in range(2)]
    grid_spec = pltpu.PrefetchScalarGridSpec(
        num_scalar_prefetch=1,
        grid=(B, H),
        in_specs=[
            pl.BlockSpec(memory_space=pltpu.SMEM),
            pl.BlockSpec(memory_space=pltpu.SMEM),
            pl.BlockSpec((1, S, HEAD_WIDTH), lambda b, h, ch: (b, 0, h)),
            pl.BlockSpec((1, S, HEAD_WIDTH), lambda b, h, ch: (b, 0, H + h), pipeline_mode=pl.Buffered(1)),
            pl.BlockSpec((1, S, HEAD_WIDTH), lambda b, h, ch: (b, 0, 2 * H + h),
                         pipeline_mode=pl.Buffered(1)),
            vec(ATTN_HEAD_DIM), vec(ATTN_HEAD_DIM), vec(ATTN_HEAD_DIM), vec(ATTN_HEAD_DIM),
            vec(HEAD_WIDTH),
        ],
        out_specs=pl.BlockSpec((1, S, HEAD_WIDTH), lambda b, h, ch: (b, 0, h)),
        scratch_shapes=[
            pltpu.VMEM((S, LANES), BF16),
            pltpu.VMEM((S, LANES), BF16),
            pltpu.VMEM((S, 2 * LANES), BF16),
            pltpu.VMEM((3, 2, tq, LANES), BF16),
            pair((tq, LANES), F32),
            pair((tq, 2 * LANES), F32),
            [pair((tq, tk), F32) for _ in range(2)],
            [pair((tq, tk), BF16) for _ in range(2)],
            [pair((tq, LANES), F32) for _ in range(2)],
        ],
    )
    return pl.pallas_call(
        kern,
        grid_spec=grid_spec,
        out_shape=jax.ShapeDtypeStruct((B, S, H * HEAD_WIDTH), BF16),
        compiler_params=_cparams(("arbitrary", "arbitrary")),
        name="diff_attention",
    )(choice, slopes, jnp.full((1,), lam_init, F32), z3, z3, z3, lq1, lk1, lq2, lk2, gsub)


def _post_kernel(o_ref, u_ref, vg_ref, ga_ref, gb_ref, x_ref, gt1_ref, sh2_ref, sc2_ref,
                 lng_ref, lnb_ref, ws_ref, bs_ref, wa_ref, wb_ref, wo_ref, gpost_ref,
                 gpre_ref, wr_ref, br_ref,
                 xo_ref, h2_ref, aff_ref, mixed_scr, *, tm):
    ya = jnp.dot(o_ref[...], wa_ref[...], preferred_element_type=F32)

    u = _gelu(u_ref[...].astype(F32))
    vg = _gelu(vg_ref[...].astype(F32))
    mu = jnp.mean(vg, axis=-1, keepdims=True)
    vc = vg - mu
    var = jnp.mean(vc * vc, axis=-1, keepdims=True)
    vgn = (vc * lax.rsqrt(var + NORM_EPS) * lng_ref[...] + lnb_ref[...]).astype(BF16)
    for c in range(tm // GMLP_CHUNK):
        rows = slice(c * GMLP_CHUNK, (c + 1) * GMLP_CHUNK)
        for g in range(GMLP_GROUPS):
            cols = slice(g * LANES, (g + 1) * LANES)
            blk = jnp.dot(ws_ref[g], vgn[rows, cols], preferred_element_type=F32)
            mixed_scr[rows, cols] = blk + bs_ref[:, g:g + 1]
    yb = jnp.dot((u * mixed_scr[...]).astype(BF16), wb_ref[...], preferred_element_type=F32)

    merged = (jax.nn.sigmoid(ga_ref[...].astype(F32)) * ya
              + jax.nn.sigmoid(gb_ref[...].astype(F32)) * yb)
    y = jnp.dot(merged.astype(BF16), wo_ref[...], preferred_element_type=F32)
    x = x_ref[...] + gt1_ref[0] * _rms(y, gpost_ref[...])
    xo_ref[...] = x

    h2 = _rms(x, gpre_ref[...]) * (1.0 + sc2_ref[0]) + sh2_ref[0]
    h2_ref[...] = h2
    logits = lax.dot_general(wr_ref[...], h2.astype(BF16), (((1,), (1,)), ((), ())),
                             preferred_element_type=F32) + br_ref[...]
    e = jnp.exp(logits - jnp.max(logits, axis=0, keepdims=True))
    aff_ref[0] = e / jnp.sum(e, axis=0, keepdims=True)


def _post(o2, z, x2, mod3, lng, lnb, ws, bsT, wa, wb, wo, gpost, gpre, wrT, br, S):
    T, D = x2.shape
    E = wrT.shape[0]
    tm = min(512, S)
    per = S // tm
    W = D
    kern = functools.partial(_post_kernel, tm=tm)
    zcol = lambda k: pl.BlockSpec((tm, W), lambda i: (i, k))
    modc = lambda k: pl.BlockSpec((1, 1, D), lambda i: (i // per, 0, k))
    full = lambda a: pl.BlockSpec(a.shape, lambda i: (0,) * a.ndim)
    return pl.pallas_call(
        kern,
        grid=(T // tm,),
        in_specs=[
            pl.BlockSpec((tm, W), lambda i: (i, 0)),
            zcol(3), zcol(4), zcol(5), zcol(6),
            pl.BlockSpec((tm, D), lambda i: (i, 0)),
            modc(2), modc(3), modc(4),
            full(lng), full(lnb), full(ws), full(bsT), full(wa), full(wb), full(wo),
            full(gpost), full(gpre), full(wrT), full(br),
        ],
        out_specs=[
            pl.BlockSpec((tm, D), lambda i: (i, 0)),
            pl.BlockSpec((tm, D), lambda i: (i, 0)),
            pl.BlockSpec((1, E, tm), lambda i: (i // per, 0, i % per)),
        ],
        out_shape=[
            jax.ShapeDtypeStruct((T, D), F32),
            jax.ShapeDtypeStruct((T, D), F32),
            jax.ShapeDtypeStruct((T // S, E, S), F32),
        ],
        scratch_shapes=[pltpu.VMEM((tm, W), F32)],
        compiler_params=_cparams(("arbitrary",)),
        name="mixer_tail",
    )(o2, z, z, z, z, x2, mod3, mod3, mod3, lng, lnb, ws, bsT, wa, wb, wo, gpost, gpre, wrT, br)


def _route_kernel(aff_ref, pos_ref, sg_ref, tok_ref, *, C):
    aff = aff_ref[0]
    E, nc, _ = aff.shape
    rows = E * nc
    bits = pltpu.bitcast(aff.reshape(rows, LANES), I32).reshape(E, nc, LANES)

    def count(mask):
        return jnp.sum(jnp.sum(mask.astype(I32), axis=2, keepdims=True), axis=1, keepdims=True)

    def step(it, thr):
        cand = thr | lax.shift_left(jnp.int32(1), 30 - it)
        return jnp.where(count(bits >= cand) >= C, cand, thr)

    thr = lax.fori_loop(0, 31, step, jnp.zeros((E, 1, 1), I32))
    gt = bits > thr
    tie = bits == thr
    need = (C - count(gt)).astype(F32)

    r = lax.broadcasted_iota(I32, (LANES, LANES), 0)
    c = lax.broadcasted_iota(I32, (LANES, LANES), 1)
    tri = jnp.where(r <= c, 1.0, 0.0).astype(BF16)
    row_id = (lax.broadcasted_iota(I32, (E, nc, rows), 0) * nc
              + lax.broadcasted_iota(I32, (E, nc, rows), 1)).reshape(rows, rows)
    first = (lax.broadcasted_iota(I32, (E, nc, rows), 0) * nc).reshape(rows, rows)
    col_id = lax.broadcasted_iota(I32, (rows, rows), 1)
    before = jnp.where((col_id < row_id) & (col_id >= first), 1.0, 0.0).astype(BF16)

    def seq_cumsum(x):
        within = jnp.dot(x.reshape(rows, LANES).astype(BF16), tri, preferred_element_type=F32)
        totals = jnp.broadcast_to(within[:, LANES - 1:LANES], (rows, LANES)).astype(BF16)
        return within, jnp.dot(before, totals, preferred_element_type=F32)

    tw, te = seq_cumsum(tie.astype(F32))
    sel = gt | (tie & ((tw + te).reshape(E, nc, LANES) <= need))
    self32 = sel.astype(F32)
    within, earlier = seq_cumsum(self32)
    pos_ref[0] = (within + earlier).reshape(E, nc, LANES).astype(I32) - sel.astype(I32)
    sg_ref[0] = jnp.where(sel, aff, 0.0)

    slot = lax.broadcasted_iota(I32, (1, C), 1).astype(F32)
    chunk_id = lax.broadcasted_iota(I32, (nc, C), 0).astype(F32)
    for e in range(E):
        w_e = within[e * nc:(e + 1) * nc, :]
        ex_e = earlier[e * nc:(e + 1) * nc, 0:1]
        upto = ex_e + w_e[:, LANES - 1:LANES]
        ch = jnp.sum(jnp.where(upto <= slot, 1.0, 0.0), axis=0, keepdims=True)
        onehot = chunk_id == ch
        local = slot - jnp.sum(jnp.where(onehot, ex_e, 0.0), axis=0, keepdims=True)
        counts = lax.dot_general(w_e.astype(BF16), jnp.where(onehot, 1.0, 0.0).astype(BF16),
                                 (((0,), (0,)), ((), ())), preferred_element_type=F32)
        place = jnp.sum(jnp.where(counts <= local, 1.0, 0.0), axis=0, keepdims=True)
        tok_ref[0, e:e + 1, :] = (ch * LANES + place).astype(I32)


def _route(aff, C):
    B, E, S = aff.shape
    nc = S // LANES
    kern = functools.partial(_route_kernel, C=C)
    blk = pl.BlockSpec((1, E, nc, LANES), lambda b: (b, 0, 0, 0))
    pos, sg, tok = pl.pallas_call(
        kern,
        grid=(B,),
        in_specs=[blk],
        out_specs=[blk, blk, pl.BlockSpec((1, E, C), lambda b: (b, 0, 0))],
        out_shape=[
            jax.ShapeDtypeStruct((B, E, nc, LANES), I32),
            jax.ShapeDtypeStruct((B, E, nc, LANES), F32),
            jax.ShapeDtypeStruct((B, E, C), I32),
        ],
        compiler_params=_cparams(("arbitrary",)),
        name="route",
    )(aff.reshape(B, E, nc, LANES))
    return pos.reshape(B, E, S), sg.reshape(B, E, S), tok.reshape(B * E, 1, C)


def _expert_kernel(tok_ref, tokn_ref, h2_hbm, wg_ref, wu_ref, wd_ref, y_ref, xbuf, xg_scr, acc_scr, sem,
                   *, S, C, E, R, nf):
    r = pl.program_id(0)
    f = pl.program_id(1)
    per = C // nf

    def row_copy(tok_smem, pair, s):
        t = tok_smem[0, 0, s]
        return pltpu.make_async_copy(h2_hbm.at[pl.ds((pair // E) * S + t, 1), :],
                                     xbuf.at[pl.ds(s, 1), :], sem)

    def wait_all_rows():
        pltpu.make_async_copy(h2_hbm.at[pl.ds(0, C), :], xbuf, sem).wait()

    @pl.when((r == 0) & (f == 0))
    def _():
        def issue(s, carry):
            row_copy(tok_ref, r, s).start()
            return carry

        lax.fori_loop(0, C, issue, 0)

    @pl.when(f == 0)
    def _():
        wait_all_rows()
        xg_scr[...] = xbuf[...].astype(BF16)
        acc_scr[...] = jnp.zeros_like(acc_scr)

    nxt = jnp.minimum(r + 1, R - 1)

    xg = xg_scr[...]
    g = jnp.dot(xg, wg_ref[0, 0].astype(BF16), preferred_element_type=F32)
    u = jnp.dot(xg, wu_ref[0, 0].astype(BF16), preferred_element_type=F32)
    hid = (g * jax.nn.sigmoid(g) * u).astype(BF16)
    acc_scr[...] += jnp.dot(hid, wd_ref[0, 0].astype(BF16), preferred_element_type=F32)
    for s in range(per):
        row_copy(tokn_ref, nxt, f * per + s).start()

    @pl.when(f == nf - 1)
    def _():
        y_ref[0] = acc_scr[...].astype(BF16)

    @pl.when((r == R - 1) & (f == nf - 1))
    def _():
        wait_all_rows()


def _experts(tok, h2, w_g, w_u, w_d, layer, B, S, C):
    _, E, D, F = w_g.shape
    tf = 512
    nf = F // tf
    R = B * E
    kern = functools.partial(_expert_kernel, S=S, C=C, E=E, R=R, nf=nf)
    return pl.pallas_call(
        kern,
        grid=(R, nf),
        in_specs=[
            pl.BlockSpec((1, 1, C), lambda r, f: (r, 0, 0), memory_space=pltpu.SMEM),
            pl.BlockSpec((1, 1, C), lambda r, f: (jnp.minimum(r + 1, R - 1), 0, 0),
                         memory_space=pltpu.SMEM),
            pl.BlockSpec(memory_space=pl.ANY),
            pl.BlockSpec((1, 1, D, tf), lambda r, f: (layer, r % E, 0, f)),
            pl.BlockSpec((1, 1, D, tf), lambda r, f: (layer, r % E, 0, f)),
            pl.BlockSpec((1, 1, tf, D), lambda r, f: (layer, r % E, f, 0)),
        ],
        out_specs=pl.BlockSpec((1, C, D), lambda r, f: (r, 0, 0)),
        out_shape=jax.ShapeDtypeStruct((R, C, D), BF16),
        scratch_shapes=[
            pltpu.VMEM((C, D), F32),
            pltpu.VMEM((C, D), BF16),
            pltpu.VMEM((C, D), F32),
            pltpu.SemaphoreType.DMA(()),
        ],
        compiler_params=_cparams(("arbitrary", "arbitrary")),
        name="experts",
    )(tok, tok, h2, w_g, w_u, w_d)


def _combine_kernel(starts_ref, y_hbm, pos_ref, sg_ref, x_ref, gt2_ref, gpost_ref, xo_ref,
                    win, sem, *, E, C, tt, W, nt, n_steps):
    b = pl.program_id(0)
    i = pl.program_id(1)
    step = b * nt + i

    par = lax.rem(step, 2)

    def copy(bb, ii, slot, e):
        s0 = starts_ref[(bb * E + e) * nt + ii]
        aligned = lax.shift_left(lax.shift_right_logical(s0, SUBLANE_SHIFT), SUBLANE_SHIFT)
        s0a = pl.multiple_of(jnp.minimum(aligned, C - W), BF16_SUBLANES)
        return s0a, pltpu.make_async_copy(y_hbm.at[bb * E + e, pl.ds(s0a, W), :], win.at[slot, e],
                                          sem.at[slot, e])

    @pl.when(step == 0)
    def _():
        for e in range(E):
            copy(b, i, par, e)[1].start()

    @pl.when(step + 1 < n_steps)
    def _():
        wrap = i + 1 == nt
        for e in range(E):
            copy(jnp.where(wrap, b + 1, b), jnp.where(wrap, 0, i + 1), 1 - par, e)[1].start()

    starts = []
    for e in range(E):
        s0a, cp = copy(b, i, par, e)
        cp.wait()
        starts.append(s0a)

    acc = jnp.zeros(xo_ref.shape, F32)
    for e in range(E):
        s0a = starts[e]
        slot = s0a + lax.broadcasted_iota(I32, (W, tt), 0)
        wt = jnp.where(pos_ref[0, e:e + 1, :] == slot, sg_ref[0, e:e + 1, :], 0.0).astype(BF16)
        acc = acc + lax.dot_general(wt, win[par, e], (((0,), (0,)), ((), ())),
                                    preferred_element_type=F32)

    xo_ref[...] = x_ref[...] + gt2_ref[0] * _rms(acc, gpost_ref[...])


def _combine_tile(C):
    return min(128, C // 2)


def _combine(starts, y, pos, sg, x2, mod3, gpost, S, C):
    T, D = x2.shape
    B, E, _ = pos.shape
    tt = _combine_tile(C)
    W = tt + BF16_SUBLANES
    nt = S // tt
    kern = functools.partial(_combine_kernel, E=E, C=C, tt=tt, W=W, nt=nt, n_steps=B * nt)
    return pl.pallas_call(
        kern,
        grid_spec=pltpu.PrefetchScalarGridSpec(
            num_scalar_prefetch=1,
            grid=(B, nt),
            in_specs=[
                pl.BlockSpec(memory_space=pl.ANY),
                pl.BlockSpec((1, E, tt), lambda b, i, st: (b, 0, i)),
                pl.BlockSpec((1, E, tt), lambda b, i, st: (b, 0, i)),
                pl.BlockSpec((tt, D), lambda b, i, st: (b * nt + i, 0)),
                pl.BlockSpec((1, 1, D), lambda b, i, st: (b, 0, 5)),
                pl.BlockSpec((1, D), lambda b, i, st: (0, 0)),
            ],
            out_specs=pl.BlockSpec((tt, D), lambda b, i, st: (b * nt + i, 0)),
            scratch_shapes=[
                pltpu.VMEM((2, E, W, D), BF16),
                pltpu.SemaphoreType.DMA((2, E)),
            ],
        ),
        out_shape=jax.ShapeDtypeStruct((T, D), F32),
        compiler_params=_cparams(("arbitrary", "arbitrary")),
        name="combine",
    )(starts, y, pos, sg, x2, mod3, gpost)


def _permute_qk(w):
    D = w.shape[0]
    return w.reshape(D, 2, ATTN_HEADS, ATTN_HEAD_DIM).transpose(0, 2, 1, 3).reshape(D, -1)


def kernel(x, c, w_ada, b_ada, g_pre_mix, g_post_mix, w_in, lam_q1, lam_k1, lam_q2, lam_k2, g_subln, ln_v_g, ln_v_b, w_spatial, b_spatial, w_branch_a, w_branch_b, w_out, g_pre_ffn, g_post_ffn, w_router, b_router, w_gate_e, w_up_e, w_down_e):
    B, S, D = x.shape
    L = w_ada.shape[0]
    E = w_router.shape[-1]
    C = EC_CAPACITY * S // E
    T = B * S
    qk = 2 * ATTN_HEADS * ATTN_HEAD_DIM

    mod = _modulation(c, w_ada, b_ada)
    slopes = 2.0 ** (-8.0 * jnp.arange(1, ATTN_HEADS + 1, dtype=F32) / ATTN_HEADS)
    x2 = x.reshape(T, D)
    row = lambda v: v.reshape(1, -1)

    for l in range(L):
        lam_init = 0.8 - 0.6 * math.exp(-0.3 * l)
        mod3 = mod[l].reshape(B, 1, 6 * D)
        w = w_in[l]
        w_l = jnp.concatenate(
            [_permute_qk(w[:, :qk]) * (LOG2E * ATTN_HEAD_DIM ** -0.5), _permute_qk(w[:, qk:2 * qk]),
             w[:, 2 * qk:]],
            axis=1).astype(BF16)

        z = _inproj(x2, mod3, row(g_pre_mix[l]), w_l, S)
        o = _attention(z.reshape(B, S, -1), slopes, row(lam_q1[l]), row(lam_k1[l]), row(lam_q2[l]),
                       row(lam_k2[l]), row(g_subln[l]), lam_init)
        x2, h2, aff = _post(
            o.reshape(T, -1), z, x2, mod3, row(ln_v_g[l]), row(ln_v_b[l]),
            w_spatial[l].astype(BF16), b_spatial[l].T, w_branch_a[l].astype(BF16),
            w_branch_b[l].astype(BF16), w_out[l].astype(BF16), row(g_post_mix[l]),
            row(g_pre_ffn[l]), w_router[l].T.astype(BF16), b_router[l].reshape(E, 1), S)

        pos, sg, tok = _route(aff, C)
        y = _experts(tok, h2, w_gate_e, w_up_e, w_down_e, l, B, S, C)
        starts = pos[:, :, ::_combine_tile(C)].reshape(-1)
        x2 = _combine(starts, y, pos, sg, x2, mod3, row(g_post_ffn[l]), S, C)

    return x2.reshape(B, S, D)
```

```python
import functools
import math

import numpy as np
import jax
import jax.numpy as jnp
from jax import lax
from jax.experimental import pallas as pl
from jax.experimental.pallas import tpu as pltpu

F32 = jnp.float32
BF16 = jnp.bfloat16
I32 = jnp.int32

NORM_EPS = 1e-6
ATTN_HEADS = 8
ATTN_HEAD_DIM = 64
HEAD_WIDTH = 2 * ATTN_HEAD_DIM
GMLP_CHUNK = 128
GMLP_GROUPS = 8
EC_CAPACITY = 2
LANES = 128
BF16_SUBLANES = 16
SUBLANE_SHIFT = 4
VMEM_LIMIT = 56 * 1024 * 1024
LOG2E = 1.4426950408889634
POS_SPLIT = 64
EXP2_ZERO = 160.0


def _bf16_split3(x):
    out = []
    r = np.float64(x)
    for _ in range(3):
        c = np.float64(np.asarray(r, np.float32).astype(BF16).astype(np.float32))
        out.append(float(c))
        r = r - c
    return out


LOG2E_BF16_PARTS = _bf16_split3(LOG2E)


def _cparams(sem):
    return pltpu.CompilerParams(dimension_semantics=sem, vmem_limit_bytes=VMEM_LIMIT)


def _rms(x, g):
    return x * lax.rsqrt(jnp.mean(x * x, axis=-1, keepdims=True) + NORM_EPS) * g


def _gelu(x):
    return 0.5 * x * (1.0 + lax.erf(x * (2.0 ** -0.5)))


def _mod_kernel(c_ref, w_ref, b_ref, o_ref):
    c = c_ref[...]
    ca = (c * jax.nn.sigmoid(c)).astype(BF16)
    o_ref[0] = jnp.dot(ca, w_ref[0].astype(BF16), preferred_element_type=F32) + b_ref[0]


def _modulation(c, w_ada, b_ada):
    L, D, W = w_ada.shape
    B = c.shape[0]
    return pl.pallas_call(
        _mod_kernel,
        grid=(L, W // D),
        in_specs=[
            pl.BlockSpec((B, D), lambda l, j: (0, 0)),
            pl.BlockSpec((1, D, D), lambda l, j: (l, 0, j)),
            pl.BlockSpec((1, 1, D), lambda l, j: (l, 0, j)),
        ],
        out_specs=pl.BlockSpec((1, B, D), lambda l, j: (l, 0, j)),
        out_shape=jax.ShapeDtypeStruct((L, B, W), F32),
        compiler_params=_cparams(("arbitrary", "arbitrary")),
        name="modulation",
    )(c, w_ada, b_ada.reshape(L, 1, W))


def _inproj_kernel(x_ref, sh_ref, sc_ref, g_ref, w_ref, z_ref, h_scr):
    @pl.when(pl.program_id(1) == 0)
    def _():
        h = _rms(x_ref[...], g_ref[...]) * (1.0 + sc_ref[0]) + sh_ref[0]
        h_scr[...] = h.astype(BF16)

    z_ref[...] = jnp.dot(h_scr[...], w_ref[...], preferred_element_type=F32).astype(BF16)


def _inproj(x2, mod3, g, w, S):
    T, D = x2.shape
    N = w.shape[1]
    tm = min(1024, S)
    tn = 1024
    per = S // tm
    return pl.pallas_call(
        _inproj_kernel,
        grid=(T // tm, N // tn),
        in_specs=[
            pl.BlockSpec((tm, D), lambda i, j: (i, 0)),
            pl.BlockSpec((1, 1, D), lambda i, j: (i // per, 0, 0)),
            pl.BlockSpec((1, 1, D), lambda i, j: (i // per, 0, 1)),
            pl.BlockSpec((1, D), lambda i, j: (0, 0)),
            pl.BlockSpec((D, tn), lambda i, j: (0, j)),
        ],
        out_specs=pl.BlockSpec((tm, tn), lambda i, j: (i, j)),
        out_shape=jax.ShapeDtypeStruct((T, N), BF16),
        scratch_shapes=[pltpu.VMEM((tm, D), BF16)],
        compiler_params=_cparams(("arbitrary", "arbitrary")),
        name="inproj",
    )(x2, mod3, mod3, g, w)


def _lane_tile(x, n):
    return jnp.concatenate([x] * n, axis=1)


def _bias_lanes(lane, base, v_hi, v_lo):
    return jnp.where((lane >= base) & (lane < base + 3), v_hi,
                     jnp.where((lane >= base + 3) & (lane < base + 6), v_lo, 0.0))


def _const_lanes(lane, base, sign):
    c = jnp.zeros(lane.shape, F32)
    for t, part in enumerate(LOG2E_BF16_PARTS):
        c = jnp.where((lane == base + t) | (lane == base + 3 + t), sign * part, c)
    return c


def _attn_kernel(slope_ref, lam0_ref, q_ref, k_ref, v_ref, lq1_ref, lk1_ref, lq2_ref, lk2_ref,
                 gsub_ref, *rest, tq, tk, S, rb, nw, head0, aliased):
    (o_ref, ka_scr, kb_scr, va_scr, qaug_scr, m_scr, acc_scr, s_scr, p_scr,
     al_scr) = rest[1:] if aliased else rest
    slope = slope_ref[head0 + pl.program_id(1)]
    slope2 = slope * LOG2E
    nk = S // tk
    nq = S // tq
    D = ATTN_HEAD_DIM
    dn = (((1,), (1,)), ((), ()))
    kx = (ka_scr, kb_scr)

    pc = min(512, S)
    for c in range(S // pc):
        rows = slice(c * pc, (c + 1) * pc)
        lane = lax.broadcasted_iota(I32, (pc, LANES), 1)
        pos = c * pc + lax.broadcasted_iota(I32, (pc, LANES), 0)
        hi = ((pos // POS_SPLIT) * POS_SPLIT).astype(F32) * slope
        lo = (pos % POS_SPLIT).astype(F32) * slope
        kk = k_ref[0, rows, :].astype(F32)
        ka_scr[rows, :] = jnp.where(lane < D, kk, _bias_lanes(lane, D, hi, lo)).astype(BF16)
        kb_scr[rows, :] = jnp.where(lane >= D, kk, _bias_lanes(lane, 0, hi, lo)).astype(BF16)
        va_scr[rows, :HEAD_WIDTH] = v_ref[0, rows, :]
        va_scr[rows, HEAD_WIDTH:] = jnp.where(lane == 0, 1.0, 0.0).astype(BF16)

    lam_init = lam0_ref[0]
    lam = (jnp.exp(jnp.sum(lq1_ref[...] * lk1_ref[...], axis=1, keepdims=True))
           - jnp.exp(jnp.sum(lq2_ref[...] * lk2_ref[...], axis=1, keepdims=True)) + lam_init)


    def tile_rows(qt):
        return pl.ds(pl.multiple_of(qt * tq, tq), tq)

    def first_chunk(qt):
        return lax.div(qt, tk // tq)

    def chunk_of(qt, t):
        jd = first_chunk(qt)
        start = jnp.clip(jd - _window_reach(nw), 0, nk - nw)
        rel = jd - start + t
        return start + jnp.where(rel >= nw, rel - nw, rel)

    def variant_of(qt, t):
        return jnp.where(chunk_of(qt, t) > first_chunk(qt), 1, 0)

    def build_q(qt):
        q = q_ref[0, tile_rows(qt), :].astype(F32)
        lane = lax.broadcasted_iota(I32, (tq, LANES), 1)
        for v, sign in enumerate((1.0, -1.0, 0.0)):
            qaug_scr[v, 0] = jnp.where(lane < D, q, _const_lanes(lane, D, sign)).astype(BF16)
            qaug_scr[v, 1] = jnp.where(lane >= D, q, _const_lanes(lane, 0, sign)).astype(BF16)

    def qk(qt, t, variant):
        ks = pl.ds(pl.multiple_of(chunk_of(qt, t) * tk, tk), tk)
        for mp in range(2):
            s_scr[t % 2][mp][...] = lax.dot_general(qaug_scr[variant, mp], kx[mp][ks, :], dn,
                                                    preferred_element_type=F32)

    def softmax(qt, t):
        par = t % 2
        c = chunk_of(qt, t)
        osign = jnp.where(c > first_chunk(qt), 1.0, -1.0).astype(F32)
        row0 = qt * tq
        for mp in range(2):
            for r in range(0, tq, rb):
                s = s_scr[par][mp][r:r + rb, :]
                if t == 0:
                    kpos = (c * tk + lax.broadcasted_iota(I32, (1, tk), 1)).astype(F32)
                    qpos = (row0 + r + lax.broadcasted_iota(I32, (rb, 1), 0)).astype(F32)
                    s = s - slope2 * jnp.abs(qpos - kpos)
                    m_new = jnp.broadcast_to(jnp.max(s, axis=1, keepdims=True), (rb, LANES))
                    shift = m_new
                    al_scr[par][mp][r:r + rb, :] = jnp.zeros((rb, LANES), F32)
                else:
                    qposr = (row0 + r + lax.broadcasted_iota(I32, (rb, LANES), 0)).astype(F32)
                    off = (osign * slope2) * qposr
                    mx = jnp.broadcast_to(jnp.max(s, axis=1, keepdims=True), (rb, LANES))
                    m_old = m_scr[mp][r:r + rb, :]
                    m_new = jnp.maximum(m_old, mx + off)
                    shift = m_new - off
                    al_scr[par][mp][r:r + rb, :] = jnp.exp2(m_old - m_new)
                p = jnp.exp2(s - _lane_tile(shift, tk // LANES))
                p_scr[par][mp][r:r + rb, :] = p.astype(BF16)
                m_scr[mp][r:r + rb, :] = m_new

    def pv(qt, t):
        par = t % 2
        ks = pl.ds(pl.multiple_of(chunk_of(qt, t) * tk, tk), tk)
        for mp in range(2):
            upd = jnp.dot(p_scr[par][mp][...], va_scr[ks, :], preferred_element_type=F32)
            if t == 0:
                acc_scr[mp][...] = upd
            else:
                acc_scr[mp][...] = _lane_tile(al_scr[par][mp][...], 2) * acc_scr[mp][...] + upd

    def finalize(qt):
        a1 = acc_scr[0][...]
        a2 = acc_scr[1][...]
        o1 = a1[:, :HEAD_WIDTH] / a1[:, HEAD_WIDTH:HEAD_WIDTH + 1]
        o2 = a2[:, :HEAD_WIDTH] / a2[:, HEAD_WIDTH:HEAD_WIDTH + 1]
        od = o1 - lam * o2
        o_ref[0, tile_rows(qt), :] = (_rms(od, gsub_ref[...]) * (1.0 - lam_init)).astype(BF16)

    def tile(qt, has_prev):
        build_q(qt)
        for t in range(nw):
            qk(qt, t, 2 if t == 0 else variant_of(qt, t))
            if t >= 1:
                softmax(qt, t - 1)
            elif has_prev:
                softmax(qt - 1, nw - 1)
            if t >= 2:
                pv(qt, t - 2)
            elif has_prev:
                pv(qt - 1, nw - 2 + t)
                if t == 1:
                    finalize(qt - 1)

    tile(jnp.int32(0), False)

    def body(qt, carry):
        tile(qt, True)
        return carry

    lax.fori_loop(1, nq, body, 0)
    softmax(nq - 1, nw - 1)
    pv(nq - 1, nw - 2)
    pv(nq - 1, nw - 1)
    finalize(nq - 1)


def _window_reach(nw):
    return (nw - 2) // 2


def _attn_tiles(S):
    tk = min(1024, S // 4)
    tq = min(512, tk)
    return tq, tk, 32


def _attention(z3, slopes, lq1, lk1, lq2, lk2, gsub, lam_init):
    B, S, _ = z3.shape
    H = ATTN_HEADS
    tq, tk, rb = _attn_tiles(S)
    nk = S // tk
    assert S % tk == 0 and nk % 2 == 0 and nk >= 4 and tk % tq == 0 and tq % rb == 0
    width = H * HEAD_WIDTH

    def max_norm(cols):
        v = cols.astype(F32).reshape(B * S, H, HEAD_WIDTH)
        return jnp.sqrt(jnp.max(jnp.sum(v * v, axis=-1), axis=0))

    spread = 2.0 * max_norm(z3[..., :width]) * max_norm(z3[..., width:2 * width]) + EXP2_ZERO
    lam0 = jnp.full((1,), lam_init, F32)
    args = (slopes, lam0, z3, lq1, lk1, lq2, lk2, gsub)

    def fits(nw, h):
        return spread[h] <= slopes[h] * LOG2E * (_window_reach(nw) * tk + 1)

    def all_chunks(*a):
        return _attention_heads(*a, None, nk, 0, H)

    plan = [(nw, h0, n) for nw, h0, n in ((4, 0, 2), (6, 2, 1)) if nw < nk]
    if not plan:
        return all_chunks(*args)
    covered = sum(n for _, _, n in plan)

    def windowed(*a):
        o = jnp.zeros((B, S, width), BF16)
        for nw, h0, n in plan:
            o = _attention_heads(*a, o, nw, h0, n)
        return _attention_heads(*a, o, nk, covered, H - covered)

    ok = jnp.bool_(True)
    for nw, h0, n in plan:
        for h in range(h0, h0 + n):
            ok = ok & fits(nw, h)
    return lax.cond(ok, windowed, all_chunks, *args)


def _attention_heads(slopes, lam0, z3, lq1, lk1, lq2, lk2, gsub, o_prev, nw, head0, n_heads):
    B, S, _ = z3.shape
    H = ATTN_HEADS
    tq, tk, rb = _attn_tiles(S)
    aliased = o_prev is not None
    kern = functools.partial(_attn_kernel, tq=tq, tk=tk, S=S, rb=rb, nw=nw, head0=head0, aliased=aliased)
    vec = lambda n: pl.BlockSpec((1, n), lambda b, h: (0, 0))
    pair = lambda shape, dt: [pltpu.VMEM(shape, dt) for _ in range(2)]
    in_specs = [
        pl.BlockSpec(memory_space=pltpu.SMEM),
        pl.BlockSpec(memory_space=pltpu.SMEM),
        pl.BlockSpec((1, S, HEAD_WIDTH), lambda b, h: (b, 0, head0 + h)),
        pl.BlockSpec((1, S, HEAD_WIDTH), lambda b, h: (b, 0, H + head0 + h), pipeline_mode=pl.Buffered(1)),
        pl.BlockSpec((1, S, HEAD_WIDTH), lambda b, h: (b, 0, 2 * H + head0 + h),
                     pipeline_mode=pl.Buffered(1)),
        vec(ATTN_HEAD_DIM), vec(ATTN_HEAD_DIM), vec(ATTN_HEAD_DIM), vec(ATTN_HEAD_DIM),
        vec(HEAD_WIDTH),
    ]
    operands = [slopes, lam0, z3, z3, z3, lq1, lk1, lq2, lk2, gsub]
    if aliased:
        in_specs.append(pl.BlockSpec(memory_space=pl.ANY))
        operands.append(o_prev)
    return pl.pallas_call(
        functools.partial(kern),
        grid=(B, n_heads),
        in_specs=in_specs,
        out_specs=pl.BlockSpec((1, S, HEAD_WIDTH), lambda b, h: (b, 0, head0 + h)),
        out_shape=jax.ShapeDtypeStruct((B, S, H * HEAD_WIDTH), BF16),
        input_output_aliases={len(operands) - 1: 0} if aliased else {},
        scratch_shapes=[
            pltpu.VMEM((S, LANES), BF16),
            pltpu.VMEM((S, LANES), BF16),
            pltpu.VMEM((S, 2 * LANES), BF16),
            pltpu.VMEM((3, 2, tq, LANES), BF16),
            pair((tq, LANES), F32),
            pair((tq, 2 * LANES), F32),
            [pair((tq, tk), F32) for _ in range(2)],
            [pair((tq, tk), BF16) for _ in range(2)],
            [pair((tq, LANES), F32) for _ in range(2)],
        ],
        compiler_params=_cparams(("arbitrary", "arbitrary")),
        name="diff_attention",
    )(*operands)


def _post_kernel(o_ref, u_ref, vg_ref, ga_ref, gb_ref, x_ref, gt1_ref, sh2_ref, sc2_ref,
                 lng_ref, lnb_ref, ws_ref, bs_ref, wa_ref, wb_ref, wo_ref, gpost_ref,
                 gpre_ref, wr_ref, br_ref,
                 xo_ref, h2_ref, aff_ref, mixed_scr, *, tm):
    ya = jnp.dot(o_ref[...], wa_ref[...], preferred_element_type=F32)

    u = _gelu(u_ref[...].astype(F32))
    vg = _gelu(vg_ref[...].astype(F32))
    mu = jnp.mean(vg, axis=-1, keepdims=True)
    vc = vg - mu
    var = jnp.mean(vc * vc, axis=-1, keepdims=True)
    vgn = (vc * lax.rsqrt(var + NORM_EPS) * lng_ref[...] + lnb_ref[...]).astype(BF16)
    for c in range(tm // GMLP_CHUNK):
        rows = slice(c * GMLP_CHUNK, (c + 1) * GMLP_CHUNK)
        for g in range(GMLP_GROUPS):
            cols = slice(g * LANES, (g + 1) * LANES)
            blk = jnp.dot(ws_ref[g], vgn[rows, cols], preferred_element_type=F32)
            mixed_scr[rows, cols] = blk + bs_ref[:, g:g + 1]
    yb = jnp.dot((u * mixed_scr[...]).astype(BF16), wb_ref[...], preferred_element_type=F32)

    merged = (jax.nn.sigmoid(ga_ref[...].astype(F32)) * ya
              + jax.nn.sigmoid(gb_ref[...].astype(F32)) * yb)
    y = jnp.dot(merged.astype(BF16), wo_ref[...], preferred_element_type=F32)
    x = x_ref[...] + gt1_ref[0] * _rms(y, gpost_ref[...])
    xo_ref[...] = x

    h2 = _rms(x, gpre_ref[...]) * (1.0 + sc2_ref[0]) + sh2_ref[0]
    h2_ref[...] = h2
    logits = lax.dot_general(wr_ref[...], h2.astype(BF16), (((1,), (1,)), ((), ())),
                             preferred_element_type=F32) + br_ref[...]
    e = jnp.exp(logits - jnp.max(logits, axis=0, keepdims=True))
    aff_ref[0] = e / jnp.sum(e, axis=0, keepdims=True)


def _post(o2, z, x2, mod3, lng, lnb, ws, bsT, wa, wb, wo, gpost, gpre, wrT, br, S):
    T, D = x2.shape
    E = wrT.shape[0]
    tm = min(512, S)
    per = S // tm
    W = D
    kern = functools.partial(_post_kernel, tm=tm)
    zcol = lambda k: pl.BlockSpec((tm, W), lambda i: (i, k))
    modc = lambda k: pl.BlockSpec((1, 1, D), lambda i: (i // per, 0, k))
    full = lambda a: pl.BlockSpec(a.shape, lambda i: (0,) * a.ndim)
    return pl.pallas_call(
        kern,
        grid=(T // tm,),
        in_specs=[
            pl.BlockSpec((tm, W), lambda i: (i, 0)),
            zcol(3), zcol(4), zcol(5), zcol(6),
            pl.BlockSpec((tm, D), lambda i: (i, 0)),
            modc(2), modc(3), modc(4),
            full(lng), full(lnb), full(ws), full(bsT), full(wa), full(wb), full(wo),
            full(gpost), full(gpre), full(wrT), full(br),
        ],
        out_specs=[
            pl.BlockSpec((tm, D), lambda i: (i, 0)),
            pl.BlockSpec((tm, D), lambda i: (i, 0)),
            pl.BlockSpec((1, E, tm), lambda i: (i // per, 0, i % per)),
        ],
        out_shape=[
            jax.ShapeDtypeStruct((T, D), F32),
            jax.ShapeDtypeStruct((T, D), F32),
            jax.ShapeDtypeStruct((T // S, E, S), F32),
        ],
        scratch_shapes=[pltpu.VMEM((tm, W), F32)],
        compiler_params=_cparams(("arbitrary",)),
        name="mixer_tail",
    )(o2, z, z, z, z, x2, mod3, mod3, mod3, lng, lnb, ws, bsT, wa, wb, wo, gpost, gpre, wrT, br)


def _route_kernel(aff_ref, pos_ref, sg_ref, tok_ref, *, C):
    aff = aff_ref[0]
    E, nc, _ = aff.shape
    rows = E * nc
    bits = pltpu.bitcast(aff.reshape(rows, LANES), I32).reshape(E, nc, LANES)

    def count(mask):
        return jnp.sum(jnp.sum(mask.astype(I32), axis=2, keepdims=True), axis=1, keepdims=True)

    def step(it, thr):
        cand = thr | lax.shift_left(jnp.int32(1), 30 - it)
        return jnp.where(count(bits >= cand) >= C, cand, thr)

    thr = lax.fori_loop(0, 31, step, jnp.zeros((E, 1, 1), I32))
    gt = bits > thr
    tie = bits == thr
    need = (C - count(gt)).astype(F32)

    r = lax.broadcasted_iota(I32, (LANES, LANES), 0)
    c = lax.broadcasted_iota(I32, (LANES, LANES), 1)
    tri = jnp.where(r <= c, 1.0, 0.0).astype(BF16)
    row_id = (lax.broadcasted_iota(I32, (E, nc, rows), 0) * nc
              + lax.broadcasted_iota(I32, (E, nc, rows), 1)).reshape(rows, rows)
    first = (lax.broadcasted_iota(I32, (E, nc, rows), 0) * nc).reshape(rows, rows)
    col_id = lax.broadcasted_iota(I32, (rows, rows), 1)
    before = jnp.where((col_id < row_id) & (col_id >= first), 1.0, 0.0).astype(BF16)

    def seq_cumsum(x):
        within = jnp.dot(x.reshape(rows, LANES).astype(BF16), tri, preferred_element_type=F32)
        totals = jnp.broadcast_to(within[:, LANES - 1:LANES], (rows, LANES)).astype(BF16)
        return within, jnp.dot(before, totals, preferred_element_type=F32)

    tw, te = seq_cumsum(tie.astype(F32))
    sel = gt | (tie & ((tw + te).reshape(E, nc, LANES) <= need))
    self32 = sel.astype(F32)
    within, earlier = seq_cumsum(self32)
    pos_ref[0] = (within + earlier).reshape(E, nc, LANES).astype(I32) - sel.astype(I32)
    sg_ref[0] = jnp.where(sel, aff, 0.0)

    slot = lax.broadcasted_iota(I32, (1, C), 1).astype(F32)
    chunk_id = lax.broadcasted_iota(I32, (nc, C), 0).astype(F32)
    for e in range(E):
        w_e = within[e * nc:(e + 1) * nc, :]
        ex_e = earlier[e * nc:(e + 1) * nc, 0:1]
        upto = ex_e + w_e[:, LANES - 1:LANES]
        ch = jnp.sum(jnp.where(upto <= slot, 1.0, 0.0), axis=0, keepdims=True)
        onehot = chunk_id == ch
        local = slot - jnp.sum(jnp.where(onehot, ex_e, 0.0), axis=0, keepdims=True)
        counts = lax.dot_general(w_e.astype(BF16), jnp.where(onehot, 1.0, 0.0).astype(BF16),
                                 (((0,), (0,)), ((), ())), preferred_element_type=F32)
        place = jnp.sum(jnp.where(counts <= local, 1.0, 0.0), axis=0, keepdims=True)
        tok_ref[0, e:e + 1, :] = (ch * LANES + place).astype(I32)


def _route(aff, C):
    B, E, S = aff.shape
    nc = S // LANES
    kern = functools.partial(_route_kernel, C=C)
    blk = pl.BlockSpec((1, E, nc, LANES), lambda b: (b, 0, 0, 0))
    pos, sg, tok = pl.pallas_call(
        kern,
        grid=(B,),
        in_specs=[blk],
        out_specs=[blk, blk, pl.BlockSpec((1, E, C), lambda b: (b, 0, 0))],
        out_shape=[
            jax.ShapeDtypeStruct((B, E, nc, LANES), I32),
            jax.ShapeDtypeStruct((B, E, nc, LANES), F32),
            jax.ShapeDtypeStruct((B, E, C), I32),
        ],
        compiler_params=_cparams(("arbitrary",)),
        name="route",
    )(aff.reshape(B, E, nc, LANES))
    return pos.reshape(B, E, S), sg.reshape(B, E, S), tok.reshape(B * E, 1, C)


def _expert_kernel(tok_ref, tokn_ref, h2_hbm, wg_ref, wu_ref, wd_ref, y_ref, xbuf, xg_scr, acc_scr, sem,
                   *, S, C, E, R, nf):
    r = pl.program_id(0)
    f = pl.program_id(1)
    per = C // nf

    def row_copy(tok_smem, pair, s):
        t = tok_smem[0, 0, s]
        return pltpu.make_async_copy(h2_hbm.at[pl.ds((pair // E) * S + t, 1), :],
                                     xbuf.at[pl.ds(s, 1), :], sem)

    def wait_all_rows():
        pltpu.make_async_copy(h2_hbm.at[pl.ds(0, C), :], xbuf, sem).wait()

    @pl.when((r == 0) & (f == 0))
    def _():
        def issue(s, carry):
            row_copy(tok_ref, r, s).start()
            return carry

        lax.fori_loop(0, C, issue, 0)

    @pl.when(f == 0)
    def _():
        wait_all_rows()
        xg_scr[...] = xbuf[...].astype(BF16)
        acc_scr[...] = jnp.zeros_like(acc_scr)

    nxt = jnp.minimum(r + 1, R - 1)

    xg = xg_scr[...]
    g = jnp.dot(xg, wg_ref[0, 0].astype(BF16), preferred_element_type=F32)
    u = jnp.dot(xg, wu_ref[0, 0].astype(BF16), preferred_element_type=F32)
    hid = (g * jax.nn.sigmoid(g) * u).astype(BF16)
    acc_scr[...] += jnp.dot(hid, wd_ref[0, 0].astype(BF16), preferred_element_type=F32)
    for s in range(per):
        row_copy(tokn_ref, nxt, f * per + s).start()

    @pl.when(f == nf - 1)
    def _():
        y_ref[0] = acc_scr[...].astype(BF16)

    @pl.when((r == R - 1) & (f == nf - 1))
    def _():
        wait_all_rows()


def _experts(tok, h2, w_g, w_u, w_d, layer, B, S, C):
    _, E, D, F = w_g.shape
    tf = 512
    nf = F // tf
    R = B * E
    kern = functools.partial(_expert_kernel, S=S, C=C, E=E, R=R, nf=nf)
    return pl.pallas_call(
        kern,
        grid=(R, nf),
        in_specs=[
            pl.BlockSpec((1, 1, C), lambda r, f: (r, 0, 0), memory_space=pltpu.SMEM),
            pl.BlockSpec((1, 1, C), lambda r, f: (jnp.minimum(r + 1, R - 1), 0, 0),
                         memory_space=pltpu.SMEM),
            pl.BlockSpec(memory_space=pl.ANY),
            pl.BlockSpec((1, 1, D, tf), lambda r, f: (layer, r % E, 0, f)),
            pl.BlockSpec((1, 1, D, tf), lambda r, f: (layer, r % E, 0, f)),
            pl.BlockSpec((1, 1, tf, D), lambda r, f: (layer, r % E, f, 0)),
        ],
        out_specs=pl.BlockSpec((1, C, D), lambda r, f: (r, 0, 0)),
        out_shape=jax.ShapeDtypeStruct((R, C, D), BF16),
        scratch_shapes=[
            pltpu.VMEM((C, D), F32),
            pltpu.VMEM((C, D), BF16),
            pltpu.VMEM((C, D), F32),
            pltpu.SemaphoreType.DMA(()),
        ],
        compiler_params=_cparams(("arbitrary", "arbitrary")),
        name="experts",
    )(tok, tok, h2, w_g, w_u, w_d)


def _combine_kernel(starts_ref, y_hbm, pos_ref, sg_ref, x_ref, gt2_ref, gpost_ref, xo_ref,
                    win, sem, *, E, C, tt, W, nt, n_steps):
    b = pl.program_id(0)
    i = pl.program_id(1)
    step = b * nt + i

    par = lax.rem(step, 2)

    def copy(bb, ii, slot, e):
        s0 = starts_ref[(bb * E + e) * nt + ii]
        aligned = lax.shift_left(lax.shift_right_logical(s0, SUBLANE_SHIFT), SUBLANE_SHIFT)
        s0a = pl.multiple_of(jnp.minimum(aligned, C - W), BF16_SUBLANES)
        return s0a, pltpu.make_async_copy(y_hbm.at[bb * E + e, pl.ds(s0a, W), :], win.at[slot, e],
                                          sem.at[slot, e])

    @pl.when(step == 0)
    def _():
        for e in range(E):
            copy(b, i, par, e)[1].start()

    @pl.when(step + 1 < n_steps)
    def _():
        wrap = i + 1 == nt
        for e in range(E):
            copy(jnp.where(wrap, b + 1, b), jnp.where(wrap, 0, i + 1), 1 - par, e)[1].start()

    starts = []
    for e in range(E):
        s0a, cp = copy(b, i, par, e)
        cp.wait()
        starts.append(s0a)

    acc = jnp.zeros(xo_ref.shape, F32)
    for e in range(E):
        s0a = starts[e]
        slot = s0a + lax.broadcasted_iota(I32, (W, tt), 0)
        wt = jnp.where(pos_ref[0, e:e + 1, :] == slot, sg_ref[0, e:e + 1, :], 0.0).astype(BF16)
        acc = acc + lax.dot_general(wt, win[par, e], (((0,), (0,)), ((), ())),
                                    preferred_element_type=F32)

    xo_ref[...] = x_ref[...] + gt2_ref[0] * _rms(acc, gpost_ref[...])


def _combine_tile(C):
    return min(128, C // 2)


def _combine(starts, y, pos, sg, x2, mod3, gpost, S, C):
    T, D = x2.shape
    B, E, _ = pos.shape
    tt = _combine_tile(C)
    W = tt + BF16_SUBLANES
    nt = S // tt
    kern = functools.partial(_combine_kernel, E=E, C=C, tt=tt, W=W, nt=nt, n_steps=B * nt)
    return pl.pallas_call(
        kern,
        grid_spec=pltpu.PrefetchScalarGridSpec(
            num_scalar_prefetch=1,
            grid=(B, nt),
            in_specs=[
                pl.BlockSpec(memory_space=pl.ANY),
                pl.BlockSpec((1, E, tt), lambda b, i, st: (b, 0, i)),
                pl.BlockSpec((1, E, tt), lambda b, i, st: (b, 0, i)),
                pl.BlockSpec((tt, D), lambda b, i, st: (b * nt + i, 0)),
                pl.BlockSpec((1, 1, D), lambda b, i, st: (b, 0, 5)),
                pl.BlockSpec((1, D), lambda b, i, st: (0, 0)),
            ],
            out_specs=pl.BlockSpec((tt, D), lambda b, i, st: (b * nt + i, 0)),
            scratch_shapes=[
                pltpu.VMEM((2, E, W, D), BF16),
                pltpu.SemaphoreType.DMA((2, E)),
            ],
        ),
        out_shape=jax.ShapeDtypeStruct((T, D), F32),
        compiler_params=_cparams(("arbitrary", "arbitrary")),
        name="combine",
    )(starts, y, pos, sg, x2, mod3, gpost)


def _permute_qk(w):
    D = w.shape[0]
    return w.reshape(D, 2, ATTN_HEADS, ATTN_HEAD_DIM).transpose(0, 2, 1, 3).reshape(D, -1)


def kernel(x, c, w_ada, b_ada, g_pre_mix, g_post_mix, w_in, lam_q1, lam_k1, lam_q2, lam_k2, g_subln, ln_v_g, ln_v_b, w_spatial, b_spatial, w_branch_a, w_branch_b, w_out, g_pre_ffn, g_post_ffn, w_router, b_router, w_gate_e, w_up_e, w_down_e):
    B, S, D = x.shape
    L = w_ada.shape[0]
    E = w_router.shape[-1]
    C = EC_CAPACITY * S // E
    T = B * S
    qk = 2 * ATTN_HEADS * ATTN_HEAD_DIM

    mod = _modulation(c, w_ada, b_ada)
    slopes = 2.0 ** (-8.0 * jnp.arange(1, ATTN_HEADS + 1, dtype=F32) / ATTN_HEADS)
    x2 = x.reshape(T, D)
    row = lambda v: v.reshape(1, -1)

    for l in range(L):
        lam_init = 0.8 - 0.6 * math.exp(-0.3 * l)
        mod3 = mod[l].reshape(B, 1, 6 * D)
        w = w_in[l]
        w_l = jnp.concatenate(
            [_permute_qk(w[:, :qk]) * (LOG2E * ATTN_HEAD_DIM ** -0.5), _permute_qk(w[:, qk:2 * qk]),
             w[:, 2 * qk:]],
            axis=1).astype(BF16)

        z = _inproj(x2, mod3, row(g_pre_mix[l]), w_l, S)
        o = _attention(z.reshape(B, S, -1), slopes, row(lam_q1[l]), row(lam_k1[l]), row(lam_q2[l]),
                       row(lam_k2[l]), row(g_subln[l]), lam_init)
        x2, h2, aff = _post(
            o.reshape(T, -1), z, x2, mod3, row(ln_v_g[l]), row(ln_v_b[l]),
            w_spatial[l].astype(BF16), b_spatial[l].T, w_branch_a[l].astype(BF16),
            w_branch_b[l].astype(BF16), w_out[l].astype(BF16), row(g_post_mix[l]),
            row(g_pre_ffn[l]), w_router[l].T.astype(BF16), b_router[l].reshape(E, 1), S)

        pos, sg, tok = _route(aff, C)
        y = _experts(tok, h2, w_gate_e, w_up_e, w_down_e, l, B, S, C)
        starts = pos[:, :, ::_combine_tile(C)].reshape(-1)
        x2 = _combine(starts, y, pos, sg, x2, mod3, row(g_post_ffn[l]), S, C)

    return x2.reshape(B, S, D)
```

```python
import functools
import math

import numpy as np
import jax
import jax.numpy as jnp
from jax import lax
from jax.experimental import pallas as pl
from jax.experimental.pallas import tpu as pltpu

F32 = jnp.float32
BF16 = jnp.bfloat16
I32 = jnp.int32

NORM_EPS = 1e-6
ATTN_HEADS = 8
ATTN_HEAD_DIM = 64
HEAD_WIDTH = 2 * ATTN_HEAD_DIM
GMLP_CHUNK = 128
GMLP_GROUPS = 8
EC_CAPACITY = 2
LANES = 128
BF16_SUBLANES = 16
SUBLANE_SHIFT = 4
VMEM_LIMIT = 56 * 1024 * 1024
LOG2E = 1.4426950408889634
POS_SPLIT = 64
EXP2_ZERO = 160.0


def _bf16_split3(x):
    out = []
    r = np.float64(x)
    for _ in range(3):
        c = np.float64(np.asarray(r, np.float32).astype(BF16).astype(np.float32))
        out.append(float(c))
        r = r - c
    return out


LOG2E_BF16_PARTS = _bf16_split3(LOG2E)


def _cparams(sem):
    return pltpu.CompilerParams(dimension_semantics=sem, vmem_limit_bytes=VMEM_LIMIT)


def _rms(x, g):
    return x * lax.rsqrt(jnp.mean(x * x, axis=-1, keepdims=True) + NORM_EPS) * g


def _gelu(x):
    return 0.5 * x * (1.0 + lax.erf(x * (2.0 ** -0.5)))


def _mod_kernel(c_ref, w_ref, b_ref, o_ref):
    c = c_ref[...]
    ca = (c * jax.nn.sigmoid(c)).astype(BF16)
    o_ref[0] = jnp.dot(ca, w_ref[0].astype(BF16), preferred_element_type=F32) + b_ref[0]


def _modulation(c, w_ada, b_ada):
    L, D, W = w_ada.shape
    B = c.shape[0]
    return pl.pallas_call(
        _mod_kernel,
        grid=(L, W // D),
        in_specs=[
            pl.BlockSpec((B, D), lambda l, j: (0, 0)),
            pl.BlockSpec((1, D, D), lambda l, j: (l, 0, j)),
            pl.BlockSpec((1, 1, D), lambda l, j: (l, 0, j)),
        ],
        out_specs=pl.BlockSpec((1, B, D), lambda l, j: (l, 0, j)),
        out_shape=jax.ShapeDtypeStruct((L, B, W), F32),
        compiler_params=_cparams(("arbitrary", "arbitrary")),
        name="modulation",
    )(c, w_ada, b_ada.reshape(L, 1, W))


def _inproj_kernel(x_ref, sh_ref, sc_ref, g_ref, w_ref, z_ref, nrm_ref, h_scr):
    j = pl.program_id(1)

    @pl.when(j == 0)
    def _():
        h = _rms(x_ref[...], g_ref[...]) * (1.0 + sc_ref[0]) + sh_ref[0]
        h_scr[...] = h.astype(BF16)

    zb = jnp.dot(h_scr[...], w_ref[...], preferred_element_type=F32).astype(BF16)
    z_ref[...] = zb

    @pl.when(j < 2)
    def _():
        zf = zb.astype(F32)
        sq = zf * zf
        for h in range(ATTN_HEADS):
            rows = jnp.sum(sq[:, h * HEAD_WIDTH:(h + 1) * HEAD_WIDTH], axis=1, keepdims=True)
            nrm_ref[0, 0, h:h + 1, :] = jnp.broadcast_to(jnp.max(rows, axis=0, keepdims=True), (1, LANES))


def _inproj(x2, mod3, g, w, S):
    T, D = x2.shape
    N = w.shape[1]
    tm = min(1024, S)
    tn = ATTN_HEADS * HEAD_WIDTH
    per = S // tm
    return pl.pallas_call(
        _inproj_kernel,
        grid=(T // tm, N // tn),
        in_specs=[
            pl.BlockSpec((tm, D), lambda i, j: (i, 0)),
            pl.BlockSpec((1, 1, D), lambda i, j: (i // per, 0, 0)),
            pl.BlockSpec((1, 1, D), lambda i, j: (i // per, 0, 1)),
            pl.BlockSpec((1, D), lambda i, j: (0, 0)),
            pl.BlockSpec((D, tn), lambda i, j: (0, j)),
        ],
        out_specs=[
            pl.BlockSpec((tm, tn), lambda i, j: (i, j)),
            pl.BlockSpec((1, 1, ATTN_HEADS, LANES), lambda i, j: (i, jnp.minimum(j, 1), 0, 0)),
        ],
        out_shape=[
            jax.ShapeDtypeStruct((T, N), BF16),
            jax.ShapeDtypeStruct((T // tm, 2, ATTN_HEADS, LANES), F32),
        ],
        scratch_shapes=[pltpu.VMEM((tm, D), BF16)],
        compiler_params=_cparams(("arbitrary", "arbitrary")),
        name="inproj",
    )(x2, mod3, mod3, g, w)


def _lane_tile(x, n):
    return jnp.concatenate([x] * n, axis=1)


def _bias_lanes(lane, base, v_hi, v_lo):
    return jnp.where((lane >= base) & (lane < base + 3), v_hi,
                     jnp.where((lane >= base + 3) & (lane < base + 6), v_lo, 0.0))


def _const_lanes(lane, base, sign):
    c = jnp.zeros(lane.shape, F32)
    for t, part in enumerate(LOG2E_BF16_PARTS):
        c = jnp.where((lane == base + t) | (lane == base + 3 + t), sign * part, c)
    return c


def _attn_kernel(slope_ref, lam0_ref, q_ref, k_ref, v_ref, lq1_ref, lk1_ref, lq2_ref, lk2_ref,
                 gsub_ref, *rest, tq, tk, S, rb, nw, head0, aliased):
    (o_ref, ka_scr, kb_scr, va_scr, qaug_scr, m_scr, acc_scr, s_scr, p_scr,
     al_scr) = rest[1:] if aliased else rest
    slope = slope_ref[head0 + pl.program_id(1)]
    slope2 = slope * LOG2E
    nk = S // tk
    nq = S // tq
    D = ATTN_HEAD_DIM
    dn = (((1,), (1,)), ((), ()))
    kx = (ka_scr, kb_scr)

    pc = min(512, S)
    for c in range(S // pc):
        rows = slice(c * pc, (c + 1) * pc)
        lane = lax.broadcasted_iota(I32, (pc, LANES), 1)
        pos = c * pc + lax.broadcasted_iota(I32, (pc, LANES), 0)
        hi = ((pos // POS_SPLIT) * POS_SPLIT).astype(F32) * slope
        lo = (pos % POS_SPLIT).astype(F32) * slope
        kk = k_ref[0, rows, :].astype(F32)
        ka_scr[rows, :] = jnp.where(lane < D, kk, _bias_lanes(lane, D, hi, lo)).astype(BF16)
        kb_scr[rows, :] = jnp.where(lane >= D, kk, _bias_lanes(lane, 0, hi, lo)).astype(BF16)
        va_scr[rows, :HEAD_WIDTH] = v_ref[0, rows, :]
        va_scr[rows, HEAD_WIDTH:] = jnp.where(lane == 0, 1.0, 0.0).astype(BF16)

    lam_init = lam0_ref[0]
    lam = (jnp.exp(jnp.sum(lq1_ref[...] * lk1_ref[...], axis=1, keepdims=True))
           - jnp.exp(jnp.sum(lq2_ref[...] * lk2_ref[...], axis=1, keepdims=True)) + lam_init)


    def tile_rows(qt):
        return pl.ds(pl.multiple_of(qt * tq, tq), tq)

    def first_chunk(qt):
        return lax.div(qt, tk // tq)

    def chunk_of(qt, t):
        jd = first_chunk(qt)
        start = jnp.clip(jd - _window_reach(nw), 0, nk - nw)
        rel = jd - start + t
        return start + jnp.where(rel >= nw, rel - nw, rel)

    def variant_of(qt, t):
        return jnp.where(chunk_of(qt, t) > first_chunk(qt), 1, 0)

    def build_q(qt):
        q = q_ref[0, tile_rows(qt), :].astype(F32)
        lane = lax.broadcasted_iota(I32, (tq, LANES), 1)
        for v, sign in enumerate((1.0, -1.0, 0.0)):
            qaug_scr[v, 0] = jnp.where(lane < D, q, _const_lanes(lane, D, sign)).astype(BF16)
            qaug_scr[v, 1] = jnp.where(lane >= D, q, _const_lanes(lane, 0, sign)).astype(BF16)

    def qk(qt, t, variant):
        ks = pl.ds(pl.multiple_of(chunk_of(qt, t) * tk, tk), tk)
        for mp in range(2):
            s_scr[t % 2][mp][...] = lax.dot_general(qaug_scr[variant, mp], kx[mp][ks, :], dn,
                                                    preferred_element_type=F32)

    def softmax(qt, t):
        par = t % 2
        c = chunk_of(qt, t)
        osign = jnp.where(c > first_chunk(qt), 1.0, -1.0).astype(F32)
        row0 = qt * tq
        for mp in range(2):
            for r in range(0, tq, rb):
                s = s_scr[par][mp][r:r + rb, :]
                if t == 0:
                    kpos = (c * tk + lax.broadcasted_iota(I32, (1, tk), 1)).astype(F32)
                    qpos = (row0 + r + lax.broadcasted_iota(I32, (rb, 1), 0)).astype(F32)
                    s = s - slope2 * jnp.abs(qpos - kpos)
                    m_new = jnp.broadcast_to(jnp.max(s, axis=1, keepdims=True), (rb, LANES))
                    shift = m_new
                    al_scr[par][mp][r:r + rb, :] = jnp.zeros((rb, LANES), F32)
                else:
                    qposr = (row0 + r + lax.broadcasted_iota(I32, (rb, LANES), 0)).astype(F32)
                    off = (osign * slope2) * qposr
                    mx = jnp.broadcast_to(jnp.max(s, axis=1, keepdims=True), (rb, LANES))
                    m_old = m_scr[mp][r:r + rb, :]
                    m_new = jnp.maximum(m_old, mx + off)
                    shift = m_new - off
                    al_scr[par][mp][r:r + rb, :] = jnp.exp2(m_old - m_new)
                p = jnp.exp2(s - _lane_tile(shift, tk // LANES))
                p_scr[par][mp][r:r + rb, :] = p.astype(BF16)
                m_scr[mp][r:r + rb, :] = m_new

    def pv(qt, t):
        par = t % 2
        ks = pl.ds(pl.multiple_of(chunk_of(qt, t) * tk, tk), tk)
        for mp in range(2):
            upd = jnp.dot(p_scr[par][mp][...], va_scr[ks, :], preferred_element_type=F32)
            if t == 0:
                acc_scr[mp][...] = upd
            else:
                acc_scr[mp][...] = _lane_tile(al_scr[par][mp][...], 2) * acc_scr[mp][...] + upd

    def finalize(qt):
        a1 = acc_scr[0][...]
        a2 = acc_scr[1][...]
        o1 = a1[:, :HEAD_WIDTH] / a1[:, HEAD_WIDTH:HEAD_WIDTH + 1]
        o2 = a2[:, :HEAD_WIDTH] / a2[:, HEAD_WIDTH:HEAD_WIDTH + 1]
        od = o1 - lam * o2
        o_ref[0, tile_rows(qt), :] = (_rms(od, gsub_ref[...]) * (1.0 - lam_init)).astype(BF16)

    def tile(qt, has_prev):
        build_q(qt)
        for t in range(nw):
            qk(qt, t, 2 if t == 0 else variant_of(qt, t))
            if t >= 1:
                softmax(qt, t - 1)
            elif has_prev:
                softmax(qt - 1, nw - 1)
            if t >= 2:
                pv(qt, t - 2)
            elif has_prev:
                pv(qt - 1, nw - 2 + t)
                if t == 1:
                    finalize(qt - 1)

    tile(jnp.int32(0), False)

    def body(qt, carry):
        tile(qt, True)
        return carry

    lax.fori_loop(1, nq, body, 0)
    softmax(nq - 1, nw - 1)
    pv(nq - 1, nw - 2)
    pv(nq - 1, nw - 1)
    finalize(nq - 1)


def _window_reach(nw):
    return (nw - 2) // 2


def _attn_tiles(S):
    tk = min(1024, S // 4)
    tq = min(512, tk)
    return tq, tk, 32


def _attention(z3, sqnorm, slopes, lq1, lk1, lq2, lk2, gsub, lam_init):
    B, S, _ = z3.shape
    H = ATTN_HEADS
    tq, tk, rb = _attn_tiles(S)
    nk = S // tk
    assert S % tk == 0 and nk % 2 == 0 and nk >= 4 and tk % tq == 0 and tq % rb == 0
    width = H * HEAD_WIDTH

    qk_norm = jnp.sqrt(jnp.max(sqnorm[..., 0], axis=0))
    spread = 2.0 * qk_norm[0] * qk_norm[1] + EXP2_ZERO
    lam0 = jnp.full((1,), lam_init, F32)
    args = (slopes, lam0, z3, lq1, lk1, lq2, lk2, gsub)

    def fits(nw, h):
        return spread[h] <= slopes[h] * LOG2E * (_window_reach(nw) * tk + 1)

    def all_chunks(*a):
        return _attention_heads(*a, None, nk, 0, H)

    plan = [(nw, h0, n) for nw, h0, n in ((4, 0, 2), (6, 2, 1)) if nw < nk]
    if not plan:
        return all_chunks(*args)
    covered = sum(n for _, _, n in plan)

    def windowed(*a):
        o = jnp.zeros((B, S, width), BF16)
        for nw, h0, n in plan:
            o = _attention_heads(*a, o, nw, h0, n)
        return _attention_heads(*a, o, nk, covered, H - covered)

    ok = jnp.bool_(True)
    for nw, h0, n in plan:
        for h in range(h0, h0 + n):
            ok = ok & fits(nw, h)
    return lax.cond(ok, windowed, all_chunks, *args)


def _attention_heads(slopes, lam0, z3, lq1, lk1, lq2, lk2, gsub, o_prev, nw, head0, n_heads):
    B, S, _ = z3.shape
    H = ATTN_HEADS
    tq, tk, rb = _attn_tiles(S)
    aliased = o_prev is not None
    kern = functools.partial(_attn_kernel, tq=tq, tk=tk, S=S, rb=rb, nw=nw, head0=head0, aliased=aliased)
    vec = lambda n: pl.BlockSpec((1, n), lambda b, h: (0, 0))
    pair = lambda shape, dt: [pltpu.VMEM(shape, dt) for _ in range(2)]
    in_specs = [
        pl.BlockSpec(memory_space=pltpu.SMEM),
        pl.BlockSpec(memory_space=pltpu.SMEM),
        pl.BlockSpec((1, S, HEAD_WIDTH), lambda b, h: (b, 0, head0 + h)),
        pl.BlockSpec((1, S, HEAD_WIDTH), lambda b, h: (b, 0, H + head0 + h), pipeline_mode=pl.Buffered(1)),
        pl.BlockSpec((1, S, HEAD_WIDTH), lambda b, h: (b, 0, 2 * H + head0 + h),
                     pipeline_mode=pl.Buffered(1)),
        vec(ATTN_HEAD_DIM), vec(ATTN_HEAD_DIM), vec(ATTN_HEAD_DIM), vec(ATTN_HEAD_DIM),
        vec(HEAD_WIDTH),
    ]
    operands = [slopes, lam0, z3, z3, z3, lq1, lk1, lq2, lk2, gsub]
    if aliased:
        in_specs.append(pl.BlockSpec(memory_space=pl.ANY))
        operands.append(o_prev)
    return pl.pallas_call(
        functools.partial(kern),
        grid=(B, n_heads),
        in_specs=in_specs,
        out_specs=pl.BlockSpec((1, S, HEAD_WIDTH), lambda b, h: (b, 0, head0 + h)),
        out_shape=jax.ShapeDtypeStruct((B, S, H * HEAD_WIDTH), BF16),
        input_output_aliases={len(operands) - 1: 0} if aliased else {},
        scratch_shapes=[
            pltpu.VMEM((S, LANES), BF16),
            pltpu.VMEM((S, LANES), BF16),
            pltpu.VMEM((S, 2 * LANES), BF16),
            pltpu.VMEM((3, 2, tq, LANES), BF16),
            pair((tq, LANES), F32),
            pair((tq, 2 * LANES), F32),
            [pair((tq, tk), F32) for _ in range(2)],
            [pair((tq, tk), BF16) for _ in range(2)],
            [pair((tq, LANES), F32) for _ in range(2)],
        ],
        compiler_params=_cparams(("arbitrary", "arbitrary")),
        name="diff_attention",
    )(*operands)


def _post_kernel(o_ref, u_ref, vg_ref, ga_ref, gb_ref, x_ref, gt1_ref, sh2_ref, sc2_ref,
                 lng_ref, lnb_ref, ws_ref, bs_ref, wa_ref, wb_ref, wo_ref, gpost_ref,
                 gpre_ref, wr_ref, br_ref,
                 xo_ref, h2_ref, aff_ref, mixed_scr, *, tm):
    ya = jnp.dot(o_ref[...], wa_ref[...], preferred_element_type=F32)

    u = _gelu(u_ref[...].astype(F32))
    vg = _gelu(vg_ref[...].astype(F32))
    mu = jnp.mean(vg, axis=-1, keepdims=True)
    vc = vg - mu
    var = jnp.mean(vc * vc, axis=-1, keepdims=True)
    vgn = (vc * lax.rsqrt(var + NORM_EPS) * lng_ref[...] + lnb_ref[...]).astype(BF16)
    for c in range(tm // GMLP_CHUNK):
        rows = slice(c * GMLP_CHUNK, (c + 1) * GMLP_CHUNK)
        for g in range(GMLP_GROUPS):
            cols = slice(g * LANES, (g + 1) * LANES)
            blk = jnp.dot(ws_ref[g], vgn[rows, cols], preferred_element_type=F32)
            mixed_scr[rows, cols] = blk + bs_ref[:, g:g + 1]
    yb = jnp.dot((u * mixed_scr[...]).astype(BF16), wb_ref[...], preferred_element_type=F32)

    merged = (jax.nn.sigmoid(ga_ref[...].astype(F32)) * ya
              + jax.nn.sigmoid(gb_ref[...].astype(F32)) * yb)
    y = jnp.dot(merged.astype(BF16), wo_ref[...], preferred_element_type=F32)
    x = x_ref[...] + gt1_ref[0] * _rms(y, gpost_ref[...])
    xo_ref[...] = x

    h2 = _rms(x, gpre_ref[...]) * (1.0 + sc2_ref[0]) + sh2_ref[0]
    h2_ref[...] = h2
    logits = lax.dot_general(wr_ref[...], h2.astype(BF16), (((1,), (1,)), ((), ())),
                             preferred_element_type=F32) + br_ref[...]
    e = jnp.exp(logits - jnp.max(logits, axis=0, keepdims=True))
    aff_ref[0] = e / jnp.sum(e, axis=0, keepdims=True)


def _post(o2, z, x2, mod3, lng, lnb, ws, bsT, wa, wb, wo, gpost, gpre, wrT, br, S):
    T, D = x2.shape
    E = wrT.shape[0]
    tm = min(512, S)
    per = S // tm
    W = D
    kern = functools.partial(_post_kernel, tm=tm)
    zcol = lambda k: pl.BlockSpec((tm, W), lambda i: (i, k))
    modc = lambda k: pl.BlockSpec((1, 1, D), lambda i: (i // per, 0, k))
    full = lambda a: pl.BlockSpec(a.shape, lambda i: (0,) * a.ndim)
    return pl.pallas_call(
        kern,
        grid=(T // tm,),
        in_specs=[
            pl.BlockSpec((tm, W), lambda i: (i, 0)),
            zcol(3), zcol(4), zcol(5), zcol(6),
            pl.BlockSpec((tm, D), lambda i: (i, 0)),
            modc(2), modc(3), modc(4),
            full(lng), full(lnb), full(ws), full(bsT), full(wa), full(wb), full(wo),
            full(gpost), full(gpre), full(wrT), full(br),
        ],
        out_specs=[
            pl.BlockSpec((tm, D), lambda i: (i, 0)),
            pl.BlockSpec((tm, D), lambda i: (i, 0)),
            pl.BlockSpec((1, E, tm), lambda i: (i // per, 0, i % per)),
        ],
        out_shape=[
            jax.ShapeDtypeStruct((T, D), F32),
            jax.ShapeDtypeStruct((T, D), F32),
            jax.ShapeDtypeStruct((T // S, E, S), F32),
        ],
        scratch_shapes=[pltpu.VMEM((tm, W), F32)],
        compiler_params=_cparams(("arbitrary",)),
        name="mixer_tail",
    )(o2, z, z, z, z, x2, mod3, mod3, mod3, lng, lnb, ws, bsT, wa, wb, wo, gpost, gpre, wrT, br)


def _route_kernel(aff_ref, pos_ref, sg_ref, tok_ref, *, C):
    aff = aff_ref[0]
    E, nc, _ = aff.shape
    rows = E * nc
    bits = pltpu.bitcast(aff.reshape(rows, LANES), I32).reshape(E, nc, LANES)

    def count(mask):
        return jnp.sum(jnp.sum(mask.astype(I32), axis=2, keepdims=True), axis=1, keepdims=True)

    def step(it, thr):
        cand = thr | lax.shift_left(jnp.int32(1), 30 - it)
        return jnp.where(count(bits >= cand) >= C, cand, thr)

    thr = lax.fori_loop(0, 31, step, jnp.zeros((E, 1, 1), I32))
    gt = bits > thr
    tie = bits == thr
    need = (C - count(gt)).astype(F32)

    r = lax.broadcasted_iota(I32, (LANES, LANES), 0)
    c = lax.broadcasted_iota(I32, (LANES, LANES), 1)
    tri = jnp.where(r <= c, 1.0, 0.0).astype(BF16)
    row_id = (lax.broadcasted_iota(I32, (E, nc, rows), 0) * nc
              + lax.broadcasted_iota(I32, (E, nc, rows), 1)).reshape(rows, rows)
    first = (lax.broadcasted_iota(I32, (E, nc, rows), 0) * nc).reshape(rows, rows)
    col_id = lax.broadcasted_iota(I32, (rows, rows), 1)
    before = jnp.where((col_id < row_id) & (col_id >= first), 1.0, 0.0).astype(BF16)

    def seq_cumsum(x):
        within = jnp.dot(x.reshape(rows, LANES).astype(BF16), tri, preferred_element_type=F32)
        totals = jnp.broadcast_to(within[:, LANES - 1:LANES], (rows, LANES)).astype(BF16)
        return within, jnp.dot(before, totals, preferred_element_type=F32)

    tw, te = seq_cumsum(tie.astype(F32))
    sel = gt | (tie & ((tw + te).reshape(E, nc, LANES) <= need))
    self32 = sel.astype(F32)
    within, earlier = seq_cumsum(self32)
    pos_ref[0] = (within + earlier).reshape(E, nc, LANES).astype(I32) - sel.astype(I32)
    sg_ref[0] = jnp.where(sel, aff, 0.0)

    slot = lax.broadcasted_iota(I32, (1, C), 1).astype(F32)
    chunk_id = lax.broadcasted_iota(I32, (nc, C), 0).astype(F32)
    for e in range(E):
        w_e = within[e * nc:(e + 1) * nc, :]
        ex_e = earlier[e * nc:(e + 1) * nc, 0:1]
        upto = ex_e + w_e[:, LANES - 1:LANES]
        ch = jnp.sum(jnp.where(upto <= slot, 1.0, 0.0), axis=0, keepdims=True)
        onehot = chunk_id == ch
        local = slot - jnp.sum(jnp.where(onehot, ex_e, 0.0), axis=0, keepdims=True)
        counts = lax.dot_general(w_e.astype(BF16), jnp.where(onehot, 1.0, 0.0).astype(BF16),
                                 (((0,), (0,)), ((), ())), preferred_element_type=F32)
        place = jnp.sum(jnp.where(counts <= local, 1.0, 0.0), axis=0, keepdims=True)
        tok_ref[0, e:e + 1, :] = (ch * LANES + place).astype(I32)


def _route(aff, C):
    B, E, S = aff.shape
    nc = S // LANES
    kern = functools.partial(_route_kernel, C=C)
    blk = pl.BlockSpec((1, E, nc, LANES), lambda b: (b, 0, 0, 0))
    pos, sg, tok = pl.pallas_call(
        kern,
        grid=(B,),
        in_specs=[blk],
        out_specs=[blk, blk, pl.BlockSpec((1, E, C), lambda b: (b, 0, 0))],
        out_shape=[
            jax.ShapeDtypeStruct((B, E, nc, LANES), I32),
            jax.ShapeDtypeStruct((B, E, nc, LANES), F32),
            jax.ShapeDtypeStruct((B, E, C), I32),
        ],
        compiler_params=_cparams(("arbitrary",)),
        name="route",
    )(aff.reshape(B, E, nc, LANES))
    return pos.reshape(B, E, S), sg.reshape(B, E, S), tok.reshape(B * E, 1, C)


def _expert_kernel(tok_ref, tokn_ref, h2_hbm, wg_ref, wu_ref, wd_ref, y_ref, xbuf, xg_scr, acc_scr, sem,
                   *, S, C, E, R, nf):
    r = pl.program_id(0)
    f = pl.program_id(1)
    per = C // nf

    def row_copy(tok_smem, pair, s):
        t = tok_smem[0, 0, s]
        return pltpu.make_async_copy(h2_hbm.at[pl.ds((pair // E) * S + t, 1), :],
                                     xbuf.at[pl.ds(s, 1), :], sem)

    def wait_all_rows():
        pltpu.make_async_copy(h2_hbm.at[pl.ds(0, C), :], xbuf, sem).wait()

    @pl.when((r == 0) & (f == 0))
    def _():
        def issue(s, carry):
            row_copy(tok_ref, r, s).start()
            return carry

        lax.fori_loop(0, C, issue, 0)

    @pl.when(f == 0)
    def _():
        wait_all_rows()
        xg_scr[...] = xbuf[...].astype(BF16)
        acc_scr[...] = jnp.zeros_like(acc_scr)

    nxt = jnp.minimum(r + 1, R - 1)

    xg = xg_scr[...]
    g = jnp.dot(xg, wg_ref[0, 0].astype(BF16), preferred_element_type=F32)
    u = jnp.dot(xg, wu_ref[0, 0].astype(BF16), preferred_element_type=F32)
    hid = (g * jax.nn.sigmoid(g) * u).astype(BF16)
    acc_scr[...] += jnp.dot(hid, wd_ref[0, 0].astype(BF16), preferred_element_type=F32)
    for s in range(per):
        row_copy(tokn_ref, nxt, f * per + s).start()

    @pl.when(f == nf - 1)
    def _():
        y_ref[0] = acc_scr[...].astype(BF16)

    @pl.when((r == R - 1) & (f == nf - 1))
    def _():
        wait_all_rows()


def _experts(tok, h2, w_g, w_u, w_d, layer, B, S, C):
    _, E, D, F = w_g.shape
    tf = 512
    nf = F // tf
    R = B * E
    kern = functools.partial(_expert_kernel, S=S, C=C, E=E, R=R, nf=nf)
    return pl.pallas_call(
        kern,
        grid=(R, nf),
        in_specs=[
            pl.BlockSpec((1, 1, C), lambda r, f: (r, 0, 0), memory_space=pltpu.SMEM),
            pl.BlockSpec((1, 1, C), lambda r, f: (jnp.minimum(r + 1, R - 1), 0, 0),
                         memory_space=pltpu.SMEM),
            pl.BlockSpec(memory_space=pl.ANY),
            pl.BlockSpec((1, 1, D, tf), lambda r, f: (layer, r % E, 0, f)),
            pl.BlockSpec((1, 1, D, tf), lambda r, f: (layer, r % E, 0, f)),
            pl.BlockSpec((1, 1, tf, D), lambda r, f: (layer, r % E, f, 0)),
        ],
        out_specs=pl.BlockSpec((1, C, D), lambda r, f: (r, 0, 0)),
        out_shape=jax.ShapeDtypeStruct((R, C, D), BF16),
        scratch_shapes=[
            pltpu.VMEM((C, D), F32),
            pltpu.VMEM((C, D), BF16),
            pltpu.VMEM((C, D), F32),
            pltpu.SemaphoreType.DMA(()),
        ],
        compiler_params=_cparams(("arbitrary", "arbitrary")),
        name="experts",
    )(tok, tok, h2, w_g, w_u, w_d)


def _combine_kernel(starts_ref, y_hbm, pos_ref, sg_ref, x_ref, gt2_ref, gpost_ref, xo_ref,
                    win, sem, *, E, C, tt, W, nt, n_steps):
    b = pl.program_id(0)
    i = pl.program_id(1)
    step = b * nt + i

    par = lax.rem(step, 2)

    def copy(bb, ii, slot, e):
        s0 = starts_ref[(bb * E + e) * nt + ii]
        aligned = lax.shift_left(lax.shift_right_logical(s0, SUBLANE_SHIFT), SUBLANE_SHIFT)
        s0a = pl.multiple_of(jnp.minimum(aligned, C - W), BF16_SUBLANES)
        return s0a, pltpu.make_async_copy(y_hbm.at[bb * E + e, pl.ds(s0a, W), :], win.at[slot, e],
                                          sem.at[slot, e])

    @pl.when(step == 0)
    def _():
        for e in range(E):
            copy(b, i, par, e)[1].start()

    @pl.when(step + 1 < n_steps)
    def _():
        wrap = i + 1 == nt
        for e in range(E):
            copy(jnp.where(wrap, b + 1, b), jnp.where(wrap, 0, i + 1), 1 - par, e)[1].start()

    starts = []
    for e in range(E):
        s0a, cp = copy(b, i, par, e)
        cp.wait()
        starts.append(s0a)

    acc = jnp.zeros(xo_ref.shape, F32)
    for e in range(E):
        s0a = starts[e]
        slot = s0a + lax.broadcasted_iota(I32, (W, tt), 0)
        wt = jnp.where(pos_ref[0, e:e + 1, :] == slot, sg_ref[0, e:e + 1, :], 0.0).astype(BF16)
        acc = acc + lax.dot_general(wt, win[par, e], (((0,), (0,)), ((), ())),
                                    preferred_element_type=F32)

    xo_ref[...] = x_ref[...] + gt2_ref[0] * _rms(acc, gpost_ref[...])


def _combine_tile(C):
    return min(128, C // 2)


def _combine(starts, y, pos, sg, x2, mod3, gpost, S, C):
    T, D = x2.shape
    B, E, _ = pos.shape
    tt = _combine_tile(C)
    W = tt + BF16_SUBLANES
    nt = S // tt
    kern = functools.partial(_combine_kernel, E=E, C=C, tt=tt, W=W, nt=nt, n_steps=B * nt)
    return pl.pallas_call(
        kern,
        grid_spec=pltpu.PrefetchScalarGridSpec(
            num_scalar_prefetch=1,
            grid=(B, nt),
            in_specs=[
                pl.BlockSpec(memory_space=pl.ANY),
                pl.BlockSpec((1, E, tt), lambda b, i, st: (b, 0, i)),
                pl.BlockSpec((1, E, tt), lambda b, i, st: (b, 0, i)),
                pl.BlockSpec((tt, D), lambda b, i, st: (b * nt + i, 0)),
                pl.BlockSpec((1, 1, D), lambda b, i, st: (b, 0, 5)),
                pl.BlockSpec((1, D), lambda b, i, st: (0, 0)),
            ],
            out_specs=pl.BlockSpec((tt, D), lambda b, i, st: (b * nt + i, 0)),
            scratch_shapes=[
                pltpu.VMEM((2, E, W, D), BF16),
                pltpu.SemaphoreType.DMA((2, E)),
            ],
        ),
        out_shape=jax.ShapeDtypeStruct((T, D), F32),
        compiler_params=_cparams(("arbitrary", "arbitrary")),
        name="combine",
    )(starts, y, pos, sg, x2, mod3, gpost)


def _permute_qk(w):
    D = w.shape[0]
    return w.reshape(D, 2, ATTN_HEADS, ATTN_HEAD_DIM).transpose(0, 2, 1, 3).reshape(D, -1)


def kernel(x, c, w_ada, b_ada, g_pre_mix, g_post_mix, w_in, lam_q1, lam_k1, lam_q2, lam_k2, g_subln, ln_v_g, ln_v_b, w_spatial, b_spatial, w_branch_a, w_branch_b, w_out, g_pre_ffn, g_post_ffn, w_router, b_router, w_gate_e, w_up_e, w_down_e):
    B, S, D = x.shape
    L = w_ada.shape[0]
    E = w_router.shape[-1]
    C = EC_CAPACITY * S // E
    T = B * S
    qk = 2 * ATTN_HEADS * ATTN_HEAD_DIM

    mod = _modulation(c, w_ada, b_ada)
    slopes = 2.0 ** (-8.0 * jnp.arange(1, ATTN_HEADS + 1, dtype=F32) / ATTN_HEADS)
    x2 = x.reshape(T, D)
    row = lambda v: v.reshape(1, -1)

    for l in range(L):
        lam_init = 0.8 - 0.6 * math.exp(-0.3 * l)
        mod3 = mod[l].reshape(B, 1, 6 * D)
        w = w_in[l]
        w_l = jnp.concatenate(
            [_permute_qk(w[:, :qk]) * (LOG2E * ATTN_HEAD_DIM ** -0.5), _permute_qk(w[:, qk:2 * qk]),
             w[:, 2 * qk:]],
            axis=1).astype(BF16)

        z, sqnorm = _inproj(x2, mod3, row(g_pre_mix[l]), w_l, S)
        o = _attention(z.reshape(B, S, -1), sqnorm, slopes, row(lam_q1[l]), row(lam_k1[l]), row(lam_q2[l]),
                       row(lam_k2[l]), row(g_subln[l]), lam_init)
        x2, h2, aff = _post(
            o.reshape(T, -1), z, x2, mod3, row(ln_v_g[l]), row(ln_v_b[l]),
            w_spatial[l].astype(BF16), b_spatial[l].T, w_branch_a[l].astype(BF16),
            w_branch_b[l].astype(BF16), w_out[l].astype(BF16), row(g_post_mix[l]),
            row(g_pre_ffn[l]), w_router[l].T.astype(BF16), b_router[l].reshape(E, 1), S)

        pos, sg, tok = _route(aff, C)
        y = _experts(tok, h2, w_gate_e, w_up_e, w_down_e, l, B, S, C)
        starts = pos[:, :, ::_combine_tile(C)].reshape(-1)
        x2 = _combine(starts, y, pos, sg, x2, mod3, row(g_post_ffn[l]), S, C)

    return x2.reshape(B, S, D)
```

```python
import functools
import math

import numpy as np
import jax
import jax.numpy as jnp
from jax import lax
from jax.experimental import pallas as pl
from jax.experimental.pallas import tpu as pltpu

F32 = jnp.float32
BF16 = jnp.bfloat16
I32 = jnp.int32

NORM_EPS = 1e-6
ATTN_HEADS = 8
ATTN_HEAD_DIM = 64
HEAD_WIDTH = 2 * ATTN_HEAD_DIM
GMLP_CHUNK = 128
GMLP_GROUPS = 8
EC_CAPACITY = 2
LANES = 128
BF16_SUBLANES = 16
SUBLANE_SHIFT = 4
VMEM_LIMIT = 56 * 1024 * 1024
LOG2E = 1.4426950408889634
POS_SPLIT = 64
EXP2_ZERO = 160.0


def _bf16_split3(x):
    out = []
    r = np.float64(x)
    for _ in range(3):
        c = np.float64(np.asarray(r, np.float32).astype(BF16).astype(np.float32))
        out.append(float(c))
        r = r - c
    return out


LOG2E_BF16_PARTS = _bf16_split3(LOG2E)


def _cparams(sem):
    return pltpu.CompilerParams(dimension_semantics=sem, vmem_limit_bytes=VMEM_LIMIT)


def _rms(x, g):
    return x * lax.rsqrt(jnp.mean(x * x, axis=-1, keepdims=True) + NORM_EPS) * g


def _gelu(x):
    return 0.5 * x * (1.0 + lax.erf(x * (2.0 ** -0.5)))


def _mod_kernel(c_ref, w_ref, b_ref, o_ref):
    c = c_ref[...]
    ca = (c * jax.nn.sigmoid(c)).astype(BF16)
    o_ref[0] = jnp.dot(ca, w_ref[0].astype(BF16), preferred_element_type=F32) + b_ref[0]


def _modulation(c, w_ada, b_ada):
    L, D, W = w_ada.shape
    B = c.shape[0]
    return pl.pallas_call(
        _mod_kernel,
        grid=(L, W // D),
        in_specs=[
            pl.BlockSpec((B, D), lambda l, j: (0, 0)),
            pl.BlockSpec((1, D, D), lambda l, j: (l, 0, j)),
            pl.BlockSpec((1, 1, D), lambda l, j: (l, 0, j)),
        ],
        out_specs=pl.BlockSpec((1, B, D), lambda l, j: (l, 0, j)),
        out_shape=jax.ShapeDtypeStruct((L, B, W), F32),
        compiler_params=_cparams(("arbitrary", "arbitrary")),
        name="modulation",
    )(c, w_ada, b_ada.reshape(L, 1, W))


def _inproj_kernel(x_ref, sh_ref, sc_ref, g_ref, w_ref, z_ref, nrm_ref, h_scr):
    j = pl.program_id(1)

    @pl.when(j == 0)
    def _():
        h = _rms(x_ref[...], g_ref[...]) * (1.0 + sc_ref[0]) + sh_ref[0]
        h_scr[...] = h.astype(BF16)

    zb = jnp.dot(h_scr[...], w_ref[...], preferred_element_type=F32).astype(BF16)
    z_ref[...] = zb

    @pl.when(j < 2)
    def _():
        zf = zb.astype(F32)
        sq = zf * zf
        for h in range(ATTN_HEADS):
            rows = jnp.sum(sq[:, h * HEAD_WIDTH:(h + 1) * HEAD_WIDTH], axis=1, keepdims=True)
            nrm_ref[0, 0, h:h + 1, :] = jnp.broadcast_to(jnp.max(rows, axis=0, keepdims=True), (1, LANES))


def _inproj(x2, mod3, g, w, S):
    T, D = x2.shape
    N = w.shape[1]
    tm = min(1024, S)
    tn = ATTN_HEADS * HEAD_WIDTH
    per = S // tm
    return pl.pallas_call(
        _inproj_kernel,
        grid=(T // tm, N // tn),
        in_specs=[
            pl.BlockSpec((tm, D), lambda i, j: (i, 0)),
            pl.BlockSpec((1, 1, D), lambda i, j: (i // per, 0, 0)),
            pl.BlockSpec((1, 1, D), lambda i, j: (i // per, 0, 1)),
            pl.BlockSpec((1, D), lambda i, j: (0, 0)),
            pl.BlockSpec((D, tn), lambda i, j: (0, j)),
        ],
        out_specs=[
            pl.BlockSpec((tm, tn), lambda i, j: (i, j)),
            pl.BlockSpec((1, 1, ATTN_HEADS, LANES), lambda i, j: (i, jnp.minimum(j, 1), 0, 0)),
        ],
        out_shape=[
            jax.ShapeDtypeStruct((T, N), BF16),
            jax.ShapeDtypeStruct((T // tm, 2, ATTN_HEADS, LANES), F32),
        ],
        scratch_shapes=[pltpu.VMEM((tm, D), BF16)],
        compiler_params=_cparams(("arbitrary", "arbitrary")),
        name="inproj",
    )(x2, mod3, mod3, g, w)


def _lane_tile(x, n):
    return jnp.concatenate([x] * n, axis=1)


def _bias_lanes(lane, base, v_hi, v_lo):
    return jnp.where((lane >= base) & (lane < base + 3), v_hi,
                     jnp.where((lane >= base + 3) & (lane < base + 6), v_lo, 0.0))


def _const_lanes(lane, base, sign):
    c = jnp.zeros(lane.shape, F32)
    for t, part in enumerate(LOG2E_BF16_PARTS):
        c = jnp.where((lane == base + t) | (lane == base + 3 + t), sign * part, c)
    return c


def _attn_kernel(slope_ref, lam0_ref, q_ref, k_ref, v_ref, lq1_ref, lk1_ref, lq2_ref, lk2_ref,
                 gsub_ref, *rest, tq, tk, S, rb, nw, head0, aliased):
    (o_ref, ka_scr, kb_scr, va_scr, qaug_scr, m_scr, acc_scr, s_scr, p_scr,
     al_scr) = rest[1:] if aliased else rest
    slope = slope_ref[head0 + pl.program_id(1)]
    slope2 = slope * LOG2E
    nk = S // tk
    nq = S // tq
    D = ATTN_HEAD_DIM
    dn = (((1,), (1,)), ((), ()))
    kx = (ka_scr, kb_scr)

    pc = min(512, S)
    for c in range(S // pc):
        rows = slice(c * pc, (c + 1) * pc)
        lane = lax.broadcasted_iota(I32, (pc, LANES), 1)
        pos = c * pc + lax.broadcasted_iota(I32, (pc, LANES), 0)
        hi = ((pos // POS_SPLIT) * POS_SPLIT).astype(F32) * slope
        lo = (pos % POS_SPLIT).astype(F32) * slope
        kk = k_ref[0, rows, :].astype(F32)
        ka_scr[rows, :] = jnp.where(lane < D, kk, _bias_lanes(lane, D, hi, lo)).astype(BF16)
        kb_scr[rows, :] = jnp.where(lane >= D, kk, _bias_lanes(lane, 0, hi, lo)).astype(BF16)
        va_scr[rows, :HEAD_WIDTH] = v_ref[0, rows, :]
        va_scr[rows, HEAD_WIDTH:] = jnp.where(lane == 0, 1.0, 0.0).astype(BF16)

    lam_init = lam0_ref[0]
    lam = (jnp.exp(jnp.sum(lq1_ref[...] * lk1_ref[...], axis=1, keepdims=True))
           - jnp.exp(jnp.sum(lq2_ref[...] * lk2_ref[...], axis=1, keepdims=True)) + lam_init)


    def tile_rows(qt):
        return pl.ds(pl.multiple_of(qt * tq, tq), tq)

    def first_chunk(qt):
        return lax.div(qt, tk // tq)

    def chunk_of(qt, t):
        jd = first_chunk(qt)
        start = jnp.clip(jd - _window_reach(nw), 0, nk - nw)
        rel = jd - start + t
        return start + jnp.where(rel >= nw, rel - nw, rel)

    def variant_of(qt, t):
        return jnp.where(chunk_of(qt, t) > first_chunk(qt), 1, 0)

    def build_q(qt):
        q = q_ref[0, tile_rows(qt), :].astype(F32)
        lane = lax.broadcasted_iota(I32, (tq, LANES), 1)
        for v, sign in enumerate((1.0, -1.0, 0.0)):
            qaug_scr[v, 0] = jnp.where(lane < D, q, _const_lanes(lane, D, sign)).astype(BF16)
            qaug_scr[v, 1] = jnp.where(lane >= D, q, _const_lanes(lane, 0, sign)).astype(BF16)

    def qk(qt, t, variant):
        ks = pl.ds(pl.multiple_of(chunk_of(qt, t) * tk, tk), tk)
        for mp in range(2):
            s_scr[t % 2][mp][...] = lax.dot_general(qaug_scr[variant, mp], kx[mp][ks, :], dn,
                                                    preferred_element_type=F32)

    def softmax(qt, t):
        par = t % 2
        c = chunk_of(qt, t)
        osign = jnp.where(c > first_chunk(qt), 1.0, -1.0).astype(F32)
        row0 = qt * tq
        for mp in range(2):
            for r in range(0, tq, rb):
                s = s_scr[par][mp][r:r + rb, :]
                if t == 0:
                    kpos = (c * tk + lax.broadcasted_iota(I32, (1, tk), 1)).astype(F32)
                    qpos = (row0 + r + lax.broadcasted_iota(I32, (rb, 1), 0)).astype(F32)
                    s = s - slope2 * jnp.abs(qpos - kpos)
                    m_new = jnp.broadcast_to(jnp.max(s, axis=1, keepdims=True), (rb, LANES))
                    shift = m_new
                    al_scr[par][mp][r:r + rb, :] = jnp.zeros((rb, LANES), F32)
                else:
                    qposr = (row0 + r + lax.broadcasted_iota(I32, (rb, LANES), 0)).astype(F32)
                    off = (osign * slope2) * qposr
                    mx = jnp.broadcast_to(jnp.max(s, axis=1, keepdims=True), (rb, LANES))
                    m_old = m_scr[mp][r:r + rb, :]
                    m_new = jnp.maximum(m_old, mx + off)
                    shift = m_new - off
                    al_scr[par][mp][r:r + rb, :] = jnp.exp2(m_old - m_new)
                p = jnp.exp2(s - _lane_tile(shift, tk // LANES))
                p_scr[par][mp][r:r + rb, :] = p.astype(BF16)
                m_scr[mp][r:r + rb, :] = m_new

    def pv(qt, t):
        par = t % 2
        ks = pl.ds(pl.multiple_of(chunk_of(qt, t) * tk, tk), tk)
        for mp in range(2):
            upd = jnp.dot(p_scr[par][mp][...], va_scr[ks, :], preferred_element_type=F32)
            if t == 0:
                acc_scr[mp][...] = upd
            else:
                acc_scr[mp][...] = _lane_tile(al_scr[par][mp][...], 2) * acc_scr[mp][...] + upd

    def finalize(qt):
        a1 = acc_scr[0][...]
        a2 = acc_scr[1][...]
        o1 = a1[:, :HEAD_WIDTH] / a1[:, HEAD_WIDTH:HEAD_WIDTH + 1]
        o2 = a2[:, :HEAD_WIDTH] / a2[:, HEAD_WIDTH:HEAD_WIDTH + 1]
        od = o1 - lam * o2
        o_ref[0, tile_rows(qt), :] = (_rms(od, gsub_ref[...]) * (1.0 - lam_init)).astype(BF16)

    def tile(qt, has_prev):
        build_q(qt)
        for t in range(nw):
            qk(qt, t, 2 if t == 0 else variant_of(qt, t))
            if t >= 1:
                softmax(qt, t - 1)
            elif has_prev:
                softmax(qt - 1, nw - 1)
            if t >= 2:
                pv(qt, t - 2)
            elif has_prev:
                pv(qt - 1, nw - 2 + t)
                if t == 1:
                    finalize(qt - 1)

    tile(jnp.int32(0), False)

    def body(qt, carry):
        tile(qt, True)
        return carry

    lax.fori_loop(1, nq, body, 0)
    softmax(nq - 1, nw - 1)
    pv(nq - 1, nw - 2)
    pv(nq - 1, nw - 1)
    finalize(nq - 1)


def _window_reach(nw):
    return (nw - 2) // 2


def _attn_tiles(S):
    tk = min(1024, S // 4)
    tq = min(512, tk)
    return tq, tk, 32


def _attention(z3, sqnorm, slopes, lq1, lk1, lq2, lk2, gsub, lam_init):
    B, S, _ = z3.shape
    H = ATTN_HEADS
    tq, tk, rb = _attn_tiles(S)
    nk = S // tk
    assert S % tk == 0 and nk % 2 == 0 and nk >= 4 and tk % tq == 0 and tq % rb == 0
    width = H * HEAD_WIDTH

    qk_norm = jnp.sqrt(jnp.max(sqnorm[..., 0], axis=0))
    spread = 2.0 * qk_norm[0] * qk_norm[1] + EXP2_ZERO
    lam0 = jnp.full((1,), lam_init, F32)
    args = (slopes, lam0, z3, lq1, lk1, lq2, lk2, gsub)

    def fits(nw, h):
        return spread[h] <= slopes[h] * LOG2E * (_window_reach(nw) * tk + 1)

    def all_chunks(*a):
        return _attention_heads(*a, None, nk, 0, H)

    plan = [(nw, h0, n) for nw, h0, n in ((4, 0, 2), (6, 2, 1)) if nw < nk]
    if not plan:
        return all_chunks(*args)
    covered = sum(n for _, _, n in plan)

    def windowed(*a):
        o = jnp.zeros((B, S, width), BF16)
        for nw, h0, n in plan:
            o = _attention_heads(*a, o, nw, h0, n)
        return _attention_heads(*a, o, nk, covered, H - covered)

    ok = jnp.bool_(True)
    for nw, h0, n in plan:
        for h in range(h0, h0 + n):
            ok = ok & fits(nw, h)
    return lax.cond(ok, windowed, all_chunks, *args)


def _attention_heads(slopes, lam0, z3, lq1, lk1, lq2, lk2, gsub, o_prev, nw, head0, n_heads):
    B, S, _ = z3.shape
    H = ATTN_HEADS
    tq, tk, rb = _attn_tiles(S)
    aliased = o_prev is not None
    kern = functools.partial(_attn_kernel, tq=tq, tk=tk, S=S, rb=rb, nw=nw, head0=head0, aliased=aliased)
    vec = lambda n: pl.BlockSpec((1, n), lambda b, h: (0, 0))
    pair = lambda shape, dt: [pltpu.VMEM(shape, dt) for _ in range(2)]
    in_specs = [
        pl.BlockSpec(memory_space=pltpu.SMEM),
        pl.BlockSpec(memory_space=pltpu.SMEM),
        pl.BlockSpec((1, S, HEAD_WIDTH), lambda b, h: (b, 0, head0 + h)),
        pl.BlockSpec((1, S, HEAD_WIDTH), lambda b, h: (b, 0, H + head0 + h), pipeline_mode=pl.Buffered(1)),
        pl.BlockSpec((1, S, HEAD_WIDTH), lambda b, h: (b, 0, 2 * H + head0 + h),
                     pipeline_mode=pl.Buffered(1)),
        vec(ATTN_HEAD_DIM), vec(ATTN_HEAD_DIM), vec(ATTN_HEAD_DIM), vec(ATTN_HEAD_DIM),
        vec(HEAD_WIDTH),
    ]
    operands = [slopes, lam0, z3, z3, z3, lq1, lk1, lq2, lk2, gsub]
    if aliased:
        in_specs.append(pl.BlockSpec(memory_space=pl.ANY))
        operands.append(o_prev)
    return pl.pallas_call(
        functools.partial(kern),
        grid=(B, n_heads),
        in_specs=in_specs,
        out_specs=pl.BlockSpec((1, S, HEAD_WIDTH), lambda b, h: (b, 0, head0 + h)),
        out_shape=jax.ShapeDtypeStruct((B, S, H * HEAD_WIDTH), BF16),
        input_output_aliases={len(operands) - 1: 0} if aliased else {},
        scratch_shapes=[
            pltpu.VMEM((S, LANES), BF16),
            pltpu.VMEM((S, LANES), BF16),
            pltpu.VMEM((S, 2 * LANES), BF16),
            pltpu.VMEM((3, 2, tq, LANES), BF16),
            pair((tq, LANES), F32),
            pair((tq, 2 * LANES), F32),
            [pair((tq, tk), F32) for _ in range(2)],
            [pair((tq, tk), BF16) for _ in range(2)],
            [pair((tq, LANES), F32) for _ in range(2)],
        ],
        compiler_params=_cparams(("arbitrary", "arbitrary")),
        name="diff_attention",
    )(*operands)


def _post_kernel(o_ref, u_ref, vg_ref, ga_ref, gb_ref, x_ref, gt1_ref, sh2_ref, sc2_ref,
                 lng_ref, lnb_ref, ws_ref, bs_ref, wa_ref, wb_ref, wo_ref, gpost_ref,
                 gpre_ref, wr_ref, br_ref,
                 xo_ref, h2_ref, aff_ref, mixed_scr, *, tm):
    ya = jnp.dot(o_ref[...], wa_ref[...], preferred_element_type=F32)

    u = _gelu(u_ref[...].astype(F32))
    vg = _gelu(vg_ref[...].astype(F32))
    mu = jnp.mean(vg, axis=-1, keepdims=True)
    vc = vg - mu
    var = jnp.mean(vc * vc, axis=-1, keepdims=True)
    vgn = (vc * lax.rsqrt(var + NORM_EPS) * lng_ref[...] + lnb_ref[...]).astype(BF16)
    for c in range(tm // GMLP_CHUNK):
        rows = slice(c * GMLP_CHUNK, (c + 1) * GMLP_CHUNK)
        for g in range(GMLP_GROUPS):
            cols = slice(g * LANES, (g + 1) * LANES)
            blk = jnp.dot(ws_ref[g], vgn[rows, cols], preferred_element_type=F32)
            mixed_scr[rows, cols] = blk + bs_ref[:, g:g + 1]
    yb = jnp.dot((u * mixed_scr[...]).astype(BF16), wb_ref[...], preferred_element_type=F32)

    merged = (jax.nn.sigmoid(ga_ref[...].astype(F32)) * ya
              + jax.nn.sigmoid(gb_ref[...].astype(F32)) * yb)
    y = jnp.dot(merged.astype(BF16), wo_ref[...], preferred_element_type=F32)
    x = x_ref[...] + gt1_ref[0] * _rms(y, gpost_ref[...])
    xo_ref[...] = x

    h2 = _rms(x, gpre_ref[...]) * (1.0 + sc2_ref[0]) + sh2_ref[0]
    h2_ref[...] = h2
    logits = lax.dot_general(wr_ref[...], h2.astype(BF16), (((1,), (1,)), ((), ())),
                             preferred_element_type=F32) + br_ref[...]
    e = jnp.exp(logits - jnp.max(logits, axis=0, keepdims=True))
    aff_ref[0] = e / jnp.sum(e, axis=0, keepdims=True)


def _post(o2, z, x2, mod3, lng, lnb, ws, bsT, wa, wb, wo, gpost, gpre, wrT, br, S):
    T, D = x2.shape
    E = wrT.shape[0]
    tm = min(512, S)
    per = S // tm
    W = D
    kern = functools.partial(_post_kernel, tm=tm)
    zcol = lambda k: pl.BlockSpec((tm, W), lambda i: (i, k))
    modc = lambda k: pl.BlockSpec((1, 1, D), lambda i: (i // per, 0, k))
    full = lambda a: pl.BlockSpec(a.shape, lambda i: (0,) * a.ndim)
    return pl.pallas_call(
        kern,
        grid=(T // tm,),
        in_specs=[
            pl.BlockSpec((tm, W), lambda i: (i, 0)),
            zcol(3), zcol(4), zcol(5), zcol(6),
            pl.BlockSpec((tm, D), lambda i: (i, 0)),
            modc(2), modc(3), modc(4),
            full(lng), full(lnb), full(ws), full(bsT), full(wa), full(wb), full(wo),
            full(gpost), full(gpre), full(wrT), full(br),
        ],
        out_specs=[
            pl.BlockSpec((tm, D), lambda i: (i, 0)),
            pl.BlockSpec((tm, D), lambda i: (i, 0)),
            pl.BlockSpec((1, E, tm), lambda i: (i // per, 0, i % per)),
        ],
        out_shape=[
            jax.ShapeDtypeStruct((T, D), F32),
            jax.ShapeDtypeStruct((T, D), F32),
            jax.ShapeDtypeStruct((T // S, E, S), F32),
        ],
        scratch_shapes=[pltpu.VMEM((tm, W), F32)],
        compiler_params=_cparams(("arbitrary",)),
        name="mixer_tail",
    )(o2, z, z, z, z, x2, mod3, mod3, mod3, lng, lnb, ws, bsT, wa, wb, wo, gpost, gpre, wrT, br)


def _route_kernel(aff_ref, pos_ref, sg_ref, tok_ref, *, C):
    aff = aff_ref[0]
    E, nc, _ = aff.shape
    rows = E * nc
    bits = pltpu.bitcast(aff.reshape(rows, LANES), I32).reshape(E, nc, LANES)

    def count(mask):
        return jnp.sum(jnp.sum(mask.astype(I32), axis=2, keepdims=True), axis=1, keepdims=True)

    def step(it, thr):
        cand = thr | lax.shift_left(jnp.int32(1), 30 - it)
        return jnp.where(count(bits >= cand) >= C, cand, thr)

    thr = lax.fori_loop(0, 31, step, jnp.zeros((E, 1, 1), I32))
    gt = bits > thr
    tie = bits == thr
    need = (C - count(gt)).astype(F32)

    r = lax.broadcasted_iota(I32, (LANES, LANES), 0)
    c = lax.broadcasted_iota(I32, (LANES, LANES), 1)
    tri = jnp.where(r <= c, 1.0, 0.0).astype(BF16)
    row_id = (lax.broadcasted_iota(I32, (E, nc, rows), 0) * nc
              + lax.broadcasted_iota(I32, (E, nc, rows), 1)).reshape(rows, rows)
    first = (lax.broadcasted_iota(I32, (E, nc, rows), 0) * nc).reshape(rows, rows)
    col_id = lax.broadcasted_iota(I32, (rows, rows), 1)
    before = jnp.where((col_id < row_id) & (col_id >= first), 1.0, 0.0).astype(BF16)

    def seq_cumsum(x):
        within = jnp.dot(x.reshape(rows, LANES).astype(BF16), tri, preferred_element_type=F32)
        totals = jnp.broadcast_to(within[:, LANES - 1:LANES], (rows, LANES)).astype(BF16)
        return within, jnp.dot(before, totals, preferred_element_type=F32)

    tw, te = seq_cumsum(tie.astype(F32))
    sel = gt | (tie & ((tw + te).reshape(E, nc, LANES) <= need))
    self32 = sel.astype(F32)
    within, earlier = seq_cumsum(self32)
    pos_ref[0] = (within + earlier).reshape(E, nc, LANES).astype(I32) - sel.astype(I32)
    sg_ref[0] = jnp.where(sel, aff, 0.0)

    slot = lax.broadcasted_iota(I32, (1, C), 1).astype(F32)
    chunk_id = lax.broadcasted_iota(I32, (nc, C), 0).astype(F32)
    for e in range(E):
        w_e = within[e * nc:(e + 1) * nc, :]
        ex_e = earlier[e * nc:(e + 1) * nc, 0:1]
        upto = ex_e + w_e[:, LANES - 1:LANES]
        ch = jnp.sum(jnp.where(upto <= slot, 1.0, 0.0), axis=0, keepdims=True)
        onehot = chunk_id == ch
        local = slot - jnp.sum(jnp.where(onehot, ex_e, 0.0), axis=0, keepdims=True)
        counts = lax.dot_general(w_e.astype(BF16), jnp.where(onehot, 1.0, 0.0).astype(BF16),
                                 (((0,), (0,)), ((), ())), preferred_element_type=F32)
        place = jnp.sum(jnp.where(counts <= local, 1.0, 0.0), axis=0, keepdims=True)
        tok_ref[0, e:e + 1, :] = (ch * LANES + place).astype(I32)


def _route(aff, C):
    B, E, S = aff.shape
    nc = S // LANES
    kern = functools.partial(_route_kernel, C=C)
    blk = pl.BlockSpec((1, E, nc, LANES), lambda b: (b, 0, 0, 0))
    pos, sg, tok = pl.pallas_call(
        kern,
        grid=(B,),
        in_specs=[blk],
        out_specs=[blk, blk, pl.BlockSpec((1, E, C), lambda b: (b, 0, 0))],
        out_shape=[
            jax.ShapeDtypeStruct((B, E, nc, LANES), I32),
            jax.ShapeDtypeStruct((B, E, nc, LANES), F32),
            jax.ShapeDtypeStruct((B, E, C), I32),
        ],
        compiler_params=_cparams(("arbitrary",)),
        name="route",
    )(aff.reshape(B, E, nc, LANES))
    return pos.reshape(B, E, S), sg.reshape(B, E, S), tok.reshape(B * E, 1, C)


def _expert_kernel(tok_ref, tokn_ref, h2_hbm, wg_ref, wu_ref, wd_ref, y_ref, xbuf, xg_scr, acc_scr, sem,
                   *, S, C, E, R, nf):
    r = pl.program_id(0)
    f = pl.program_id(1)
    per = C // nf

    def row_copy(tok_smem, pair, s):
        t = tok_smem[0, 0, s]
        return pltpu.make_async_copy(h2_hbm.at[pl.ds((pair // E) * S + t, 1), :],
                                     xbuf.at[pl.ds(s, 1), :], sem)

    def wait_all_rows():
        pltpu.make_async_copy(h2_hbm.at[pl.ds(0, C), :], xbuf, sem).wait()

    @pl.when((r == 0) & (f == 0))
    def _():
        def issue(s, carry):
            row_copy(tok_ref, r, s).start()
            return carry

        lax.fori_loop(0, C, issue, 0)

    @pl.when(f == 0)
    def _():
        wait_all_rows()
        xg_scr[...] = xbuf[...].astype(BF16)
        acc_scr[...] = jnp.zeros_like(acc_scr)

    nxt = jnp.minimum(r + 1, R - 1)

    xg = xg_scr[...]
    g = jnp.dot(xg, wg_ref[0, 0].astype(BF16), preferred_element_type=F32)
    u = jnp.dot(xg, wu_ref[0, 0].astype(BF16), preferred_element_type=F32)
    hid = (g * jax.nn.sigmoid(g) * u).astype(BF16)
    acc_scr[...] += jnp.dot(hid, wd_ref[0, 0].astype(BF16), preferred_element_type=F32)
    for s in range(per):
        row_copy(tokn_ref, nxt, f * per + s).start(priority=s % 2)

    @pl.when(f == nf - 1)
    def _():
        y_ref[0] = acc_scr[...].astype(BF16)

    @pl.when((r == R - 1) & (f == nf - 1))
    def _():
        wait_all_rows()


def _experts(tok, h2, w_g, w_u, w_d, layer, B, S, C):
    _, E, D, F = w_g.shape
    tf = 512
    nf = F // tf
    R = B * E
    kern = functools.partial(_expert_kernel, S=S, C=C, E=E, R=R, nf=nf)
    return pl.pallas_call(
        kern,
        grid=(R, nf),
        in_specs=[
            pl.BlockSpec((1, 1, C), lambda r, f: (r, 0, 0), memory_space=pltpu.SMEM),
            pl.BlockSpec((1, 1, C), lambda r, f: (jnp.minimum(r + 1, R - 1), 0, 0),
                         memory_space=pltpu.SMEM),
            pl.BlockSpec(memory_space=pl.ANY),
            pl.BlockSpec((1, 1, D, tf), lambda r, f: (layer, r % E, 0, f)),
            pl.BlockSpec((1, 1, D, tf), lambda r, f: (layer, r % E, 0, f)),
            pl.BlockSpec((1, 1, tf, D), lambda r, f: (layer, r % E, f, 0)),
        ],
        out_specs=pl.BlockSpec((1, C, D), lambda r, f: (r, 0, 0)),
        out_shape=jax.ShapeDtypeStruct((R, C, D), BF16),
        scratch_shapes=[
            pltpu.VMEM((C, D), F32),
            pltpu.VMEM((C, D), BF16),
            pltpu.VMEM((C, D), F32),
            pltpu.SemaphoreType.DMA(()),
        ],
        compiler_params=_cparams(("arbitrary", "arbitrary")),
        name="experts",
    )(tok, tok, h2, w_g, w_u, w_d)


def _combine_kernel(starts_ref, y_hbm, pos_ref, sg_ref, x_ref, gt2_ref, gpost_ref, xo_ref,
                    win, sem, *, E, C, tt, W, nt, n_steps):
    b = pl.program_id(0)
    i = pl.program_id(1)
    step = b * nt + i

    par = lax.rem(step, 2)

    def copy(bb, ii, slot, e):
        s0 = starts_ref[(bb * E + e) * nt + ii]
        aligned = lax.shift_left(lax.shift_right_logical(s0, SUBLANE_SHIFT), SUBLANE_SHIFT)
        s0a = pl.multiple_of(jnp.minimum(aligned, C - W), BF16_SUBLANES)
        return s0a, pltpu.make_async_copy(y_hbm.at[bb * E + e, pl.ds(s0a, W), :], win.at[slot, e],
                                          sem.at[slot, e])

    @pl.when(step == 0)
    def _():
        for e in range(E):
            copy(b, i, par, e)[1].start()

    @pl.when(step + 1 < n_steps)
    def _():
        wrap = i + 1 == nt
        for e in range(E):
            copy(jnp.where(wrap, b + 1, b), jnp.where(wrap, 0, i + 1), 1 - par, e)[1].start()

    starts = []
    for e in range(E):
        s0a, cp = copy(b, i, par, e)
        cp.wait()
        starts.append(s0a)

    acc = jnp.zeros(xo_ref.shape, F32)
    for e in range(E):
        s0a = starts[e]
        slot = s0a + lax.broadcasted_iota(I32, (W, tt), 0)
        wt = jnp.where(pos_ref[0, e:e + 1, :] == slot, sg_ref[0, e:e + 1, :], 0.0).astype(BF16)
        acc = acc + lax.dot_general(wt, win[par, e], (((0,), (0,)), ((), ())),
                                    preferred_element_type=F32)

    xo_ref[...] = x_ref[...] + gt2_ref[0] * _rms(acc, gpost_ref[...])


def _combine_tile(C):
    return min(128, C // 2)


def _combine(starts, y, pos, sg, x2, mod3, gpost, S, C):
    T, D = x2.shape
    B, E, _ = pos.shape
    tt = _combine_tile(C)
    W = tt + BF16_SUBLANES
    nt = S // tt
    kern = functools.partial(_combine_kernel, E=E, C=C, tt=tt, W=W, nt=nt, n_steps=B * nt)
    return pl.pallas_call(
        kern,
        grid_spec=pltpu.PrefetchScalarGridSpec(
            num_scalar_prefetch=1,
            grid=(B, nt),
            in_specs=[
                pl.BlockSpec(memory_space=pl.ANY),
                pl.BlockSpec((1, E, tt), lambda b, i, st: (b, 0, i)),
                pl.BlockSpec((1, E, tt), lambda b, i, st: (b, 0, i)),
                pl.BlockSpec((tt, D), lambda b, i, st: (b * nt + i, 0)),
                pl.BlockSpec((1, 1, D), lambda b, i, st: (b, 0, 5)),
                pl.BlockSpec((1, D), lambda b, i, st: (0, 0)),
            ],
            out_specs=pl.BlockSpec((tt, D), lambda b, i, st: (b * nt + i, 0)),
            scratch_shapes=[
                pltpu.VMEM((2, E, W, D), BF16),
                pltpu.SemaphoreType.DMA((2, E)),
            ],
        ),
        out_shape=jax.ShapeDtypeStruct((T, D), F32),
        compiler_params=_cparams(("arbitrary", "arbitrary")),
        name="combine",
    )(starts, y, pos, sg, x2, mod3, gpost)


def _permute_qk(w):
    D = w.shape[0]
    return w.reshape(D, 2, ATTN_HEADS, ATTN_HEAD_DIM).transpose(0, 2, 1, 3).reshape(D, -1)


def kernel(x, c, w_ada, b_ada, g_pre_mix, g_post_mix, w_in, lam_q1, lam_k1, lam_q2, lam_k2, g_subln, ln_v_g, ln_v_b, w_spatial, b_spatial, w_branch_a, w_branch_b, w_out, g_pre_ffn, g_post_ffn, w_router, b_router, w_gate_e, w_up_e, w_down_e):
    B, S, D = x.shape
    L = w_ada.shape[0]
    E = w_router.shape[-1]
    C = EC_CAPACITY * S // E
    T = B * S
    qk = 2 * ATTN_HEADS * ATTN_HEAD_DIM

    mod = _modulation(c, w_ada, b_ada)
    slopes = 2.0 ** (-8.0 * jnp.arange(1, ATTN_HEADS + 1, dtype=F32) / ATTN_HEADS)
    x2 = x.reshape(T, D)
    row = lambda v: v.reshape(1, -1)

    for l in range(L):
        lam_init = 0.8 - 0.6 * math.exp(-0.3 * l)
        mod3 = mod[l].reshape(B, 1, 6 * D)
        w = w_in[l]
        w_l = jnp.concatenate(
            [_permute_qk(w[:, :qk]) * (LOG2E * ATTN_HEAD_DIM ** -0.5), _permute_qk(w[:, qk:2 * qk]),
             w[:, 2 * qk:]],
            axis=1).astype(BF16)

        z, sqnorm = _inproj(x2, mod3, row(g_pre_mix[l]), w_l, S)
        o = _attention(z.reshape(B, S, -1), sqnorm, slopes, row(lam_q1[l]), row(lam_k1[l]), row(lam_q2[l]),
                       row(lam_k2[l]), row(g_subln[l]), lam_init)
        x2, h2, aff = _post(
            o.reshape(T, -1), z, x2, mod3, row(ln_v_g[l]), row(ln_v_b[l]),
            w_spatial[l].astype(BF16), b_spatial[l].T, w_branch_a[l].astype(BF16),
            w_branch_b[l].astype(BF16), w_out[l].astype(BF16), row(g_post_mix[l]),
            row(g_pre_ffn[l]), w_router[l].T.astype(BF16), b_router[l].reshape(E, 1), S)

        pos, sg, tok = _route(aff, C)
        y = _experts(tok, h2, w_gate_e, w_up_e, w_down_e, l, B, S, C)
        starts = pos[:, :, ::_combine_tile(C)].reshape(-1)
        x2 = _combine(starts, y, pos, sg, x2, mod3, row(g_post_ffn[l]), S, C)

    return x2.reshape(B, S, D)
```
